```python
import math
import jax, jax.numpy as jnp
from jax import lax
import numpy as np

D_MODEL = 1024
BATCH = 8
SEQ = 8192
DEPTH = 2
DEC_BATCH = 16
DEC_SEQ = 32
PAST_LEN = 4096

CHUNK = 64
QBLOCK = 128
N_A = DEPTH // 2
N_B = DEPTH - N_A
H_A = 8
DK_A = D_MODEL // (2 * H_A)
DV_A = D_MODEL // H_A
QK_A = H_A * DK_A
V_A = H_A * DV_A
IN_A = 2 * QK_A + 2 * V_A + 2 * H_A
H_B = 8
HD_B = D_MODEL // (2 * H_B)
K_B = H_B * 2 * HD_B
V_B = H_B * 2 * HD_B
Q_B = H_B * 2 * HD_B
D_FF = 4 * D_MODEL
EPS = 1e-6
NEG_INF = -1e30

kernel_name = 'yoco_mlstm_diffattn_stream_step'


def rmsnorm(x, g):
    xf = x.astype(jnp.float32)
    y = xf * lax.rsqrt(jnp.mean(xf * xf, axis=-1, keepdims=True) + EPS)
    return (y * g.astype(jnp.float32)).astype(x.dtype)


def sqrelu_mlp(x, w_up, w_down):
    return jnp.square(jax.nn.relu(x @ w_up)) @ w_down


def lambda_init(layer_idx):
    return 0.8 - 0.6 * math.exp(-0.3 * layer_idx)


def mlstm_chunk(carry, inp):
    c0, n0, m0 = carry
    q, k, v, li, lf = inp
    seg = q.shape[1]
    fcum = jnp.cumsum(lf, axis=1).transpose(0, 2, 1)
    li_t = li.transpose(0, 2, 1)
    causal = jnp.tril(jnp.ones((seg, seg), dtype=bool))
    dmat = jnp.where(causal, fcum[..., :, None] - fcum[..., None, :] + li_t[..., None, :], NEG_INF)
    a = fcum + m0[..., None]
    m = jnp.maximum(a, jnp.max(dmat, axis=-1))
    s = jnp.einsum('blhd,bshd->bhls', q, k) * jnp.exp(dmat - m[..., None])
    inter = jnp.exp(a - m)
    num = jnp.einsum('bhls,bshe->bhle', s, v) + inter[..., None] * jnp.einsum('blhd,bhde->bhle', q, c0)
    qn = jnp.sum(s, axis=-1) + inter * jnp.einsum('blhd,bhd->bhl', q, n0)
    h = num / jnp.maximum(jnp.abs(qn), jnp.exp(-m))[..., None]
    m_last = m[..., -1]
    w_last = jnp.exp(dmat[..., -1, :] - m_last[..., None])
    decay = jnp.exp(a[..., -1] - m_last)
    c1 = decay[..., None, None] * c0 + jnp.einsum('bhs,bshd,bshe->bhde', w_last, k, v)
    n1 = decay[..., None] * n0 + jnp.einsum('bhs,bshd->bhd', w_last, k)
    return (c1, n1, m_last), h.transpose(0, 2, 1, 3)


def mlstm_mixer(xn, w_in, b_i, b_f, g_h, w_out, state0, seg):
    bsz, seqlen, _ = xn.shape
    proj = xn @ w_in
    q, k, v, o, ig, fg = jnp.split(proj, [QK_A, 2 * QK_A, 2 * QK_A + V_A, 2 * QK_A + 2 * V_A,
                                          2 * QK_A + 2 * V_A + H_A], axis=-1)
    f32 = jnp.float32
    q = q.astype(f32).reshape(bsz, seqlen, H_A, DK_A)
    k = k.astype(f32).reshape(bsz, seqlen, H_A, DK_A) * (DK_A ** -0.5)
    v = v.astype(f32).reshape(bsz, seqlen, H_A, DV_A)
    li = ig.astype(f32) + b_i.astype(f32)
    lf = jax.nn.log_sigmoid(fg.astype(f32) + b_f.astype(f32))
    nc = seqlen // seg

    def to_chunks(t):
        return jnp.moveaxis(t.reshape((bsz, nc, seg) + t.shape[2:]), 1, 0)

    carry0 = (state0[0].astype(f32), state0[1].astype(f32), state0[2].astype(f32))
    state, h = lax.scan(mlstm_chunk, carry0, (to_chunks(q), to_chunks(k), to_chunks(v), to_chunks(li), to_chunks(lf)))
    h = jnp.moveaxis(h, 0, 1).reshape(bsz, seqlen, H_A, DV_A)
    h = rmsnorm(h, g_h.reshape(H_A, DV_A)).reshape(bsz, seqlen, V_A) * jax.nn.sigmoid(o.astype(f32))
    return h.astype(xn.dtype) @ w_out, state


def diff_attn_core(q, k, v, qpos, kpos, lam):
    s = jnp.einsum('bqhcd,bkhcd->bchqk', q.astype(jnp.float32), k.astype(jnp.float32)) * (HD_B ** -0.5)
    allowed = (kpos[None, :] // CHUNK) <= (qpos[:, None] // CHUNK)
    p = jax.nn.softmax(jnp.where(allowed, s, NEG_INF), axis=-1)
    w = p[:, 0] - lam * p[:, 1]
    return jnp.einsum('bhqk,bkhe->bqhe', w, v.astype(jnp.float32))


def diff_attn_mixer(xn, k, v, w_q, lq1, lk1, lq2, lk2, g_sub, w_o, lam_init, qpos, kpos, blocked):
    bsz, seqlen, _ = xn.shape
    f32 = jnp.float32
    q = (xn @ w_q).reshape(bsz, seqlen, H_B, 2, HD_B)
    lam = (jnp.exp(jnp.sum(lq1.astype(f32) * lk1.astype(f32)))
           - jnp.exp(jnp.sum(lq2.astype(f32) * lk2.astype(f32))) + lam_init)
    if blocked:
        nb = seqlen // QBLOCK
        qb = jnp.moveaxis(q.reshape(bsz, nb, QBLOCK, H_B, 2, HD_B), 1, 0)
        pb = qpos.reshape(nb, QBLOCK)
        o = lax.map(lambda blk: diff_attn_core(blk[0], k, v, blk[1], kpos, lam), (qb, pb))
        o = jnp.moveaxis(o, 0, 1).reshape(bsz, seqlen, H_B, 2 * HD_B)
    else:
        o = diff_attn_core(q, k, v, qpos, kpos, lam)
    o = rmsnorm(o, g_sub.reshape(H_B, 2 * HD_B)) * (1.0 - lam_init)
    return o.reshape(bsz, seqlen, V_B).astype(xn.dtype) @ w_o


def run_trunk(x, states0, past_k, past_v, seg, blocked, p):
    bsz, seqlen, _ = x.shape
    past = 0 if past_k is None else past_k.shape[1]
    qpos = past + jnp.arange(seqlen)
    kpos = jnp.arange(past + seqlen)
    c_list, n_list, m_list = [], [], []
    k_new = v_new = k_all = v_all = None
    for l in range(DEPTH):
        h = rmsnorm(x, p['g_mix_pre'][l])
        if l < N_A:
            h, (c1, n1, m1) = mlstm_mixer(h, p['w_in_a'][l], p['b_i_a'][l], p['b_f_a'][l], p['g_h_a'][l],
                                          p['w_out_a'][l], (states0[0][l], states0[1][l], states0[2][l]), seg)
            c_list.append(c1)
            n_list.append(n1)
            m_list.append(m1)
        else:
            if l == N_A:
                kv = rmsnorm(x, p['g_kv']) @ p['w_kv']
                k_new = kv[..., :K_B].reshape(bsz, seqlen, H_B, 2, HD_B)
                v_new = kv[..., K_B:].reshape(bsz, seqlen, H_B, 2 * HD_B)
                if past_k is None:
                    k_all, v_all = k_new, v_new
                else:
                    k_all = jnp.concatenate([past_k.astype(k_new.dtype), k_new], axis=1)
                    v_all = jnp.concatenate([past_v.astype(v_new.dtype), v_new], axis=1)
            j = l - N_A
            h = diff_attn_mixer(h, k_all, v_all, p['w_q_b'][j], p['lambda_q1'][j], p['lambda_k1'][j],
                                p['lambda_q2'][j], p['lambda_k2'][j], p['g_sub_b'][j], p['w_o_b'][j],
                                lambda_init(l), qpos, kpos, blocked)
        x = x + rmsnorm(h, p['g_mix_post'][l])
        f = sqrelu_mlp(rmsnorm(x, p['g_ffn_pre'][l]), p['w_up'][l], p['w_down'][l])
        x = x + rmsnorm(f, p['g_ffn_post'][l])
    return x, (jnp.stack(c_list), jnp.stack(n_list), jnp.stack(m_list)), k_new, v_new


def setup_inputs(seed: int = 0) -> dict:
    key = jax.random.key(seed)
    ks = jax.random.split(key, 32)

    def nrm(k, shape, scale):
        return jax.random.normal(k, shape, jnp.float32) * scale

    return {
        'x_prompt': nrm(ks[0], (BATCH, SEQ, D_MODEL), 1.0),
        'x_sample': nrm(ks[1], (DEC_BATCH, DEC_SEQ, D_MODEL), 1.0),
        'cache_k': nrm(ks[2], (DEC_BATCH, PAST_LEN, H_B, 2, HD_B), 1.0),
        'cache_v': nrm(ks[3], (DEC_BATCH, PAST_LEN, H_B, 2 * HD_B), 1.0),
        'state_c': nrm(ks[4], (N_A, DEC_BATCH, H_A, DK_A, DV_A), 0.5),
        'state_n': nrm(ks[5], (N_A, DEC_BATCH, H_A, DK_A), 0.5),
        'state_m': nrm(ks[6], (N_A, DEC_BATCH, H_A), 1.0),
        'w_in_a': nrm(ks[7], (N_A, D_MODEL, IN_A), D_MODEL ** -0.5),
        'b_i_a': nrm(ks[8], (N_A, H_A), 0.1),
        'b_f_a': 3.0 + nrm(ks[9], (N_A, H_A), 0.5),
        'g_h_a': 1.0 + nrm(ks[10], (N_A, V_A), 0.02),
        'w_out_a': nrm(ks[11], (N_A, V_A, D_MODEL), V_A ** -0.5),
        'g_kv': 1.0 + nrm(ks[12], (D_MODEL,), 0.02),
        'w_kv': nrm(ks[13], (D_MODEL, K_B + V_B), D_MODEL ** -0.5),
        'w_q_b': nrm(ks[14], (N_B, D_MODEL, Q_B), D_MODEL ** -0.5),
        'lambda_q1': nrm(ks[15], (N_B, HD_B), 0.1),
        'lambda_k1': nrm(ks[16], (N_B, HD_B), 0.1),
        'lambda_q2': nrm(ks[17], (N_B, HD_B), 0.1),
        'lambda_k2': nrm(ks[18], (N_B, HD_B), 0.1),
        'g_sub_b': 1.0 + nrm(ks[19], (N_B, V_B), 0.02),
        'w_o_b': nrm(ks[20], (N_B, V_B, D_MODEL), V_B ** -0.5),
        'g_mix_pre': 1.0 + nrm(ks[21], (DEPTH, D_MODEL), 0.02),
        'g_mix_post': 1.0 + nrm(ks[22], (DEPTH, D_MODEL), 0.02),
        'g_ffn_pre': 1.0 + nrm(ks[23], (DEPTH, D_MODEL), 0.02),
        'g_ffn_post': 1.0 + nrm(ks[24], (DEPTH, D_MODEL), 0.02),
        'w_up': nrm(ks[25], (DEPTH, D_MODEL, D_FF), D_MODEL ** -0.5),
        'w_down': nrm(ks[26], (DEPTH, D_FF, D_MODEL), D_FF ** -0.5),
    }


def reference(x_prompt, x_sample, cache_k, cache_v, state_c, state_n, state_m,
              w_in_a, b_i_a, b_f_a, g_h_a, w_out_a, g_kv, w_kv, w_q_b,
              lambda_q1, lambda_k1, lambda_q2, lambda_k2, g_sub_b, w_o_b,
              g_mix_pre, g_mix_post, g_ffn_pre, g_ffn_post, w_up, w_down):
    p = {'w_in_a': w_in_a, 'b_i_a': b_i_a, 'b_f_a': b_f_a, 'g_h_a': g_h_a, 'w_out_a': w_out_a,
         'g_kv': g_kv, 'w_kv': w_kv, 'w_q_b': w_q_b, 'lambda_q1': lambda_q1, 'lambda_k1': lambda_k1,
         'lambda_q2': lambda_q2, 'lambda_k2': lambda_k2, 'g_sub_b': g_sub_b, 'w_o_b': w_o_b,
         'g_mix_pre': g_mix_pre, 'g_mix_post': g_mix_post, 'g_ffn_pre': g_ffn_pre,
         'g_ffn_post': g_ffn_post, 'w_up': w_up, 'w_down': w_down}
    bsz = x_prompt.shape[0]
    zero_state = (jnp.zeros((N_A, bsz, H_A, DK_A, DV_A), jnp.float32),
                  jnp.zeros((N_A, bsz, H_A, DK_A), jnp.float32),
                  jnp.zeros((N_A, bsz, H_A), jnp.float32))
    y_prompt, (c_p, n_p, m_p), k_p, v_p = run_trunk(x_prompt, zero_state, None, None, CHUNK, True, p)
    y_sample, (c_s, n_s, m_s), k_s, v_s = run_trunk(x_sample, (state_c, state_n, state_m), cache_k, cache_v,
                                                    x_sample.shape[1], False, p)
    return (y_prompt, y_sample, c_p, n_p, m_p, k_p, v_p, c_s, n_s, m_s, k_s, v_s)
```

```python
import functools
import math

import jax
import jax.numpy as jnp
from jax import lax
from jax.experimental import pallas as pl
from jax.experimental.pallas import tpu as pltpu

F32 = jnp.float32
BF16 = jnp.bfloat16

EPS = 1e-6
NEG_INF = -1e30
CHUNK = 64
LANES = 128
H_A = 8
H_B = 8
HD_B = 64
MLSTM_BLOCK = 128
ATTN_TILE = 256
ROW_TILE = 512
FF_TILE = 1024
SAMPLE_KEY_TILE = 512
VMEM_LIMIT_BYTES = 48 * 1024 * 1024


def _params(*sem):
    return pltpu.CompilerParams(dimension_semantics=sem, vmem_limit_bytes=VMEM_LIMIT_BYTES)


def _rms(x, g):
    return x * lax.rsqrt(jnp.mean(x * x, axis=-1, keepdims=True) + EPS) * g


def _log_sigmoid(x):
    return jnp.minimum(x, 0.0) - jnp.log1p(jnp.exp(-jnp.abs(x)))


def _dot_nt(a, b):
    return lax.dot_general(a, b, (((1,), (1,)), ((), ())), preferred_element_type=F32)


def _dot_tn(a, b):
    return lax.dot_general(a, b, (((0,), (0,)), ((), ())), preferred_element_type=F32)


def _norm_proj_kernel(x_ref, g_ref, *refs, specs):
    n_w = len(specs)
    w_refs, o_refs = refs[:n_w], refs[n_w:]
    xn = _rms(x_ref[...], g_ref[...]).astype(BF16)
    oi = 0
    for w_ref, (scale, dtypes) in zip(w_refs, specs):
        acc = jnp.dot(xn, w_ref[...], preferred_element_type=F32)
        if scale != 1.0:
            acc = acc * scale
        for dt in dtypes:
            o_refs[oi][...] = acc.astype(dt)
            oi += 1


def _norm_proj(x, g, ws, specs, tm):
    m, d = x.shape
    assert m % tm == 0
    out_shapes, out_specs = [], []
    for w, (_, dtypes) in zip(ws, specs):
        for dt in dtypes:
            out_shapes.append(jax.ShapeDtypeStruct((m, w.shape[1]), dt))
            out_specs.append(pl.BlockSpec((tm, w.shape[1]), lambda i: (i, 0)))
    in_specs = [pl.BlockSpec((tm, d), lambda i: (i, 0)), pl.BlockSpec((1, d), lambda i: (0, 0))]
    in_specs += [pl.BlockSpec(w.shape, lambda i: (0, 0)) for w in ws]
    return pl.pallas_call(
        functools.partial(_norm_proj_kernel, specs=tuple(specs)),
        grid=(m // tm,),
        in_specs=in_specs,
        out_specs=out_specs,
        out_shape=out_shapes,
        compiler_params=_params("parallel"),
        name="norm_proj",
    )(x, g, *ws)


def _mm_norm_res_kernel(h_ref, w_ref, g_ref, x_ref, o_ref):
    acc = jnp.dot(h_ref[...], w_ref[...], preferred_element_type=F32)
    o_ref[...] = x_ref[...] + _rms(acc, g_ref[...])


def _mm_norm_res(h, w, g, x, tm):
    m, k = h.shape
    n = w.shape[1]
    assert m % tm == 0
    return pl.pallas_call(
        _mm_norm_res_kernel,
        grid=(m // tm,),
        in_specs=[pl.BlockSpec((tm, k), lambda i: (i, 0)),
                  pl.BlockSpec((k, n), lambda i: (0, 0)),
                  pl.BlockSpec((1, n), lambda i: (0, 0)),
                  pl.BlockSpec((tm, n), lambda i: (i, 0))],
        out_specs=pl.BlockSpec((tm, n), lambda i: (i, 0)),
        out_shape=jax.ShapeDtypeStruct((m, n), F32),
        compiler_params=_params("parallel"),
        name="mm_norm_res",
    )(h, w, g, x)


def _mlp_kernel(x_ref, gpre_ref, wup_ref, wdn_ref, gpost_ref, o_ref, xn_sc, acc_sc):
    j = pl.program_id(1)

    @pl.when(j == 0)
    def _():
        xn_sc[...] = _rms(x_ref[...], gpre_ref[...]).astype(BF16)
        acc_sc[...] = jnp.zeros_like(acc_sc)

    h = jnp.maximum(jnp.dot(xn_sc[...], wup_ref[...], preferred_element_type=F32), 0.0)
    acc_sc[...] += jnp.dot((h * h).astype(BF16), wdn_ref[...], preferred_element_type=F32)

    @pl.when(j == pl.num_programs(1) - 1)
    def _():
        o_ref[...] = x_ref[...] + _rms(acc_sc[...], gpost_ref[...])


def _mlp(x, g_pre, w_up, w_down, g_post, tm, tf):
    m, d = x.shape
    ff = w_up.shape[1]
    assert m % tm == 0 and ff % tf == 0
    return pl.pallas_call(
        _mlp_kernel,
        grid=(m // tm, ff // tf),
        in_specs=[pl.BlockSpec((tm, d), lambda i, j: (i, 0)),
                  pl.BlockSpec((1, d), lambda i, j: (0, 0)),
                  pl.BlockSpec((d, tf), lambda i, j: (0, j)),
                  pl.BlockSpec((tf, d), lambda i, j: (j, 0)),
                  pl.BlockSpec((1, d), lambda i, j: (0, 0))],
        out_specs=pl.BlockSpec((tm, d), lambda i, j: (i, 0)),
        out_shape=jax.ShapeDtypeStruct((m, d), F32),
        scratch_shapes=[pltpu.VMEM((tm, d), BF16), pltpu.VMEM((tm, d), F32)],
        compiler_params=_params("parallel", "arbitrary"),
        name="mlp",
    )(x, g_pre, w_up, w_down, g_post)


def _mlstm_kernel(q_ref, k_ref, v_ref, o_ref, gate_ref, brow_ref, bcol_ref, gh_ref, c0_ref, n0_ref, m0_ref,
                  hg_ref, c_ref, n_ref, m_ref, *, blk, heads, dk, dv):
    t = pl.program_id(1)

    @pl.when(t == 0)
    def _():
        c_ref[...] = c0_ref[...]
        n_ref[...] = n0_ref[...]
        m_ref[...] = m0_ref[...]

    gates = gate_ref[...]
    z_col = gates + brow_ref[...]
    lf_col = _log_sigmoid(z_col)
    row = lax.broadcasted_iota(jnp.int32, (blk, blk), 0)
    col = lax.broadcasted_iota(jnp.int32, (blk, blk), 1)
    causal = row >= col
    tril = causal.astype(F32)
    fcum_col = jnp.dot(tril, lf_col, preferred_element_type=F32, precision=lax.Precision.HIGHEST)

    if blk < LANES:
        gates_sq = jnp.concatenate([gates, jnp.zeros((LANES - blk, LANES), F32)], axis=0)
    else:
        gates_sq = gates
    gates_row = gates_sq.T[:, :blk]
    z_row = gates_row[0:2 * heads, :] + bcol_ref[...]
    li_row = z_row[0:heads, :]
    lf_row = _log_sigmoid(z_row[heads:2 * heads, :])
    triu = (row <= col).astype(F32)
    fcum_row = jnp.dot(lf_row, triu, preferred_element_type=F32, precision=lax.Precision.HIGHEST)
    r_row = li_row - fcum_row

    for h in range(heads):
        qh = q_ref[:, h * dk:(h + 1) * dk]
        kh = k_ref[:, h * dk:(h + 1) * dk]
        vh = v_ref[:, h * dv:(h + 1) * dv]
        c0 = c_ref[0, h]
        n0 = n_ref[0, h:h + 1, :]
        m0 = m_ref[0, h:h + 1, 0:1]

        fc = fcum_col[:, heads + h:heads + h + 1]
        li_c = z_col[:, h:h + 1]
        dmat = jnp.where(causal, fc + r_row[h:h + 1, :], NEG_INF)
        a = fc + m0
        m_new = jnp.maximum(a, jnp.max(dmat, axis=1, keepdims=True))
        s = _dot_nt(qh, kh) * jnp.exp(dmat - m_new)
        inter = jnp.exp(a - m_new)
        num = (jnp.dot(s.astype(BF16), vh, preferred_element_type=F32)
               + inter * jnp.dot(qh, c0.astype(BF16), preferred_element_type=F32))
        qn = (jnp.sum(s, axis=1, keepdims=True)
              + inter * jnp.sum(qh.astype(F32) * n0, axis=1, keepdims=True))
        hh = num / jnp.maximum(jnp.abs(qn), jnp.exp(-m_new))

        hn = _rms(hh, gh_ref[:, h * dv:(h + 1) * dv])
        og = o_ref[:, h * dv:(h + 1) * dv].astype(F32)
        hg_ref[:, h * dv:(h + 1) * dv] = (hn * jax.nn.sigmoid(og)).astype(BF16)

        m_last = m_new[blk - 1:blk, :]
        w_last = jnp.exp(fc[blk - 1:blk, :] - fc + li_c - m_last)
        decay = jnp.exp(a[blk - 1:blk, :] - m_last)
        kw = kh.astype(F32) * w_last
        c_ref[0, h] = decay * c0 + _dot_tn(kw.astype(BF16), vh)
        n_ref[0, h:h + 1, :] = decay * n0 + jnp.sum(kw, axis=0, keepdims=True)
        m_ref[0, h:h + 1, :] = jnp.broadcast_to(m_last, (1, LANES))


def _mlstm(q, k, v, o, gates, b_row, b_col, g_h, c0, n0, m0, bsz, blk):
    t_rows = q.shape[0]
    heads, dk, dv = c0.shape[1], c0.shape[2], c0.shape[3]
    seqlen = t_rows // bsz
    assert seqlen % blk == 0
    nblk = seqlen // blk
    row_map = lambda b, t: (b * nblk + t, 0)
    const2 = lambda b, t: (0, 0)
    st4 = lambda b, t: (b, 0, 0, 0)
    st3 = lambda b, t: (b, 0, 0)
    return pl.pallas_call(
        functools.partial(_mlstm_kernel, blk=blk, heads=heads, dk=dk, dv=dv),
        grid=(bsz, nblk),
        in_specs=[pl.BlockSpec((blk, heads * dk), row_map),
                  pl.BlockSpec((blk, heads * dk), row_map),
                  pl.BlockSpec((blk, heads * dv), row_map),
                  pl.BlockSpec((blk, heads * dv), row_map),
                  pl.BlockSpec((blk, LANES), row_map),
                  pl.BlockSpec((1, LANES), const2),
                  pl.BlockSpec((2 * heads, 1), const2),
                  pl.BlockSpec((1, heads * dv), const2),
                  pl.BlockSpec((1, heads, dk, dv), st4),
                  pl.BlockSpec((1, heads, dk), st3),
                  pl.BlockSpec((1, heads, LANES), st3)],
        out_specs=[pl.BlockSpec((blk, heads * dv), row_map),
                   pl.BlockSpec((1, heads, dk, dv), st4),
                   pl.BlockSpec((1, heads, dk), st3),
                   pl.BlockSpec((1, heads, LANES), st3)],
        out_shape=[jax.ShapeDtypeStruct((t_rows, heads * dv), BF16),
                   jax.ShapeDtypeStruct((bsz, heads, dk, dv), F32),
                   jax.ShapeDtypeStruct((bsz, heads, dk), F32),
                   jax.ShapeDtypeStruct((bsz, heads, LANES), F32)],
        compiler_params=_params("parallel", "arbitrary"),
        name="mlstm",
    )(q, k, v, o, gates, b_row, b_col, g_h, c0, n0, m0)


def _stack_maps(q):
    lane = lax.broadcasted_iota(jnp.int32, q.shape, 1)
    zero = jnp.zeros_like(q)
    return jnp.concatenate([jnp.where(lane < HD_B, q, zero), jnp.where(lane >= HD_B, q, zero)], axis=0)


def _lambda(lam_ref, lam_init):
    lv = lam_ref[...]
    a = jnp.sum(lv[0:1] * lv[1:2], axis=1, keepdims=True)
    b = jnp.sum(lv[2:3] * lv[3:4], axis=1, keepdims=True)
    return jnp.exp(a) - jnp.exp(b) + lam_init


def _softmax_step(s, v, m_prev, l_prev, acc_prev):
    m_new = jnp.maximum(m_prev, jnp.max(s, axis=1, keepdims=True))
    alpha = jnp.exp(m_prev - m_new)
    p = jnp.exp(s - m_new)
    l_new = alpha * l_prev + jnp.sum(p, axis=1, keepdims=True)
    acc_new = alpha * acc_prev + jnp.dot(p.astype(BF16), v, preferred_element_type=F32)
    return m_new, l_new, acc_new


def _finish_head(acc, l, lam, g_sub, lam_init, tq):
    w = acc / l
    o = w[0:tq] - lam * w[tq:2 * tq]
    return _rms(o, g_sub) * (1.0 - lam_init)


def _attn_prompt_kernel(q_ref, k_ref, v_ref, lam_ref, gsub_ref, o_ref, qs_sc, m_sc, l_sc, acc_sc, *, tile, lam_init):
    i = pl.program_id(2)
    qs_sc[...] = _stack_maps(q_ref[0])
    m_sc[...] = jnp.full(m_sc.shape, NEG_INF, F32)
    l_sc[...] = jnp.zeros_like(l_sc)
    acc_sc[...] = jnp.zeros_like(acc_sc)

    def step(j, masked):
        start = pl.multiple_of(j * tile, tile)
        kt = k_ref[0, pl.ds(start, tile), :]
        vt = v_ref[0, pl.ds(start, tile), :]
        s = _dot_nt(qs_sc[...], kt)
        if masked:
            row = lax.broadcasted_iota(jnp.int32, s.shape, 0) & (tile - 1)
            col = lax.broadcasted_iota(jnp.int32, s.shape, 1)
            s = jnp.where((col // CHUNK) <= (row // CHUNK), s, NEG_INF)
        m_new, l_new, acc_new = _softmax_step(s, vt, m_sc[...], l_sc[...], acc_sc[...])
        m_sc[...] = m_new
        l_sc[...] = l_new
        acc_sc[...] = acc_new

    def body(j, carry):
        step(j, False)
        return carry

    lax.fori_loop(0, i, body, 0)
    step(i, True)
    o = _finish_head(acc_sc[...], l_sc[...], _lambda(lam_ref, lam_init), gsub_ref[...], lam_init, tile)
    o_ref[0] = o.astype(BF16)


def _attn_prompt(q, k, v, lam_vec, g_sub, lam_init, tile):
    bsz, seqlen, width = q.shape
    heads = width // LANES
    assert seqlen % tile == 0 and tile % CHUNK == 0 and (tile & (tile - 1)) == 0
    return pl.pallas_call(
        functools.partial(_attn_prompt_kernel, tile=tile, lam_init=lam_init),
        grid=(bsz, heads, seqlen // tile),
        in_specs=[pl.BlockSpec((1, tile, LANES), lambda b, h, i: (b, i, h)),
                  pl.BlockSpec((1, seqlen, LANES), lambda b, h, i: (b, 0, h)),
                  pl.BlockSpec((1, seqlen, LANES), lambda b, h, i: (b, 0, h)),
                  pl.BlockSpec((4, HD_B), lambda b, h, i: (0, 0)),
                  pl.BlockSpec((1, LANES), lambda b, h, i: (0, h))],
        out_specs=pl.BlockSpec((1, tile, LANES), lambda b, h, i: (b, i, h)),
        out_shape=jax.ShapeDtypeStruct((bsz, seqlen, width), BF16),
        scratch_shapes=[pltpu.VMEM((2 * tile, LANES), BF16),
                        pltpu.VMEM((2 * tile, 1), F32),
                        pltpu.VMEM((2 * tile, 1), F32),
                        pltpu.VMEM((2 * tile, LANES), F32)],
        compiler_params=_params("parallel", "parallel", "arbitrary"),
        name="attn_prompt",
    )(q, k, v, lam_vec, g_sub)


def _attn_sample_kernel(q_ref, ck_ref, cv_ref, kn_ref, vn_ref, lam_ref, gsub_ref, o_ref, qs_sc, m_sc, l_sc, acc_sc,
                        *, lq, heads, past, lam_init):
    j = pl.program_id(1)

    @pl.when(j == 0)
    def _():
        for h in range(heads):
            qs_sc[h] = _stack_maps(q_ref[0, :, h * LANES:(h + 1) * LANES])
        m_sc[...] = jnp.full(m_sc.shape, NEG_INF, F32)
        l_sc[...] = jnp.zeros_like(l_sc)
        acc_sc[...] = jnp.zeros_like(acc_sc)

    def update(h, kt, vt, mask):
        s = _dot_nt(qs_sc[h], kt)
        if mask is not None:
            s = jnp.where(mask, s, NEG_INF)
        m_new, l_new, acc_new = _softmax_step(s, vt, m_sc[h], l_sc[h], acc_sc[h])
        m_sc[h] = m_new
        l_sc[h] = l_new
        acc_sc[h] = acc_new

    for h in range(heads):
        update(h, ck_ref[0, :, h * LANES:(h + 1) * LANES].astype(BF16),
               cv_ref[0, :, h * LANES:(h + 1) * LANES].astype(BF16), None)

    @pl.when(j == pl.num_programs(1) - 1)
    def _():
        row = lax.broadcasted_iota(jnp.int32, (2 * lq, lq), 0)
        row = jnp.where(row >= lq, row - lq, row)
        col = lax.broadcasted_iota(jnp.int32, (2 * lq, lq), 1)
        mask = ((past + col) // CHUNK) <= ((past + row) // CHUNK)
        lam = _lambda(lam_ref, lam_init)
        for h in range(heads):
            hs = slice(h * LANES, (h + 1) * LANES)
            update(h, kn_ref[0, :, hs], vn_ref[0, :, hs], mask)
            o = _finish_head(acc_sc[h], l_sc[h], lam, gsub_ref[:, hs], lam_init, lq)
            o_ref[0, :, hs] = o.astype(BF16)


def _attn_sample(q, cache_k, cache_v, k_new, v_new, lam_vec, g_sub, lam_init, tkc):
    bsz, lq, width = q.shape
    past = cache_k.shape[1]
    heads = width // LANES
    assert past % tkc == 0 and past % CHUNK == 0
    return pl.pallas_call(
        functools.partial(_attn_sample_kernel, lq=lq, heads=heads, past=past, lam_init=lam_init),
        grid=(bsz, past // tkc),
        in_specs=[pl.BlockSpec((1, lq, width), lambda b, j: (b, 0, 0)),
                  pl.BlockSpec((1, tkc, width), lambda b, j: (b, j, 0)),
                  pl.BlockSpec((1, tkc, width), lambda b, j: (b, j, 0)),
                  pl.BlockSpec((1, lq, width), lambda b, j: (b, 0, 0)),
                  pl.BlockSpec((1, lq, width), lambda b, j: (b, 0, 0)),
                  pl.BlockSpec((4, HD_B), lambda b, j: (0, 0)),
                  pl.BlockSpec((1, width), lambda b, j: (0, 0))],
        out_specs=pl.BlockSpec((1, lq, width), lambda b, j: (b, 0, 0)),
        out_shape=jax.ShapeDtypeStruct((bsz, lq, width), BF16),
        scratch_shapes=[pltpu.VMEM((heads, 2 * lq, LANES), BF16),
                        pltpu.VMEM((heads, 2 * lq, 1), F32),
                        pltpu.VMEM((heads, 2 * lq, 1), F32),
                        pltpu.VMEM((heads, 2 * lq, LANES), F32)],
        compiler_params=_params("parallel", "arbitrary"),
        name="attn_sample",
    )(q, cache_k, cache_v, k_new, v_new, lam_vec, g_sub)


def _lambda_init(layer_idx):
    return 0.8 - 0.6 * math.exp(-0.3 * layer_idx)


def _trunk(x3, state, cache, p, mlstm_blk):
    bsz, seqlen, d = x3.shape
    rows = bsz * seqlen
    tm = min(ROW_TILE, rows)
    x = x3.reshape(rows, d)
    row = lambda g: g.reshape(1, -1)

    qk_w = H_A * p["dk"]
    v_w = H_A * p["dv"]
    w_in = p["w_in_a"]
    w_parts = [w_in[:, 0:qk_w], w_in[:, qk_w:2 * qk_w], w_in[:, 2 * qk_w:2 * qk_w + v_w],
               w_in[:, 2 * qk_w + v_w:2 * qk_w + 2 * v_w]]
    w_gate = jnp.pad(w_in[:, 2 * qk_w + 2 * v_w:], ((0, 0), (0, LANES - 2 * H_A)))
    ws = [w.astype(BF16) for w in w_parts] + [w_gate.astype(BF16)]
    specs = [(1.0, (BF16,)), (p["dk"] ** -0.5, (BF16,)), (1.0, (BF16,)), (1.0, (BF16,)), (1.0, (F32,))]
    q, k, v, o, gates = _norm_proj(x, row(p["g_mix_pre"][0]), ws, specs, tm)

    bias = jnp.concatenate([p["b_i_a"], p["b_f_a"]]).astype(F32)
    b_row = jnp.pad(bias, (0, LANES - 2 * H_A)).reshape(1, LANES)
    b_col = bias.reshape(2 * H_A, 1)
    c0, n0, m0 = state
    m0b = jnp.broadcast_to(m0[:, :, None], (bsz, H_A, LANES))
    hg, c1, n1, m1 = _mlstm(q, k, v, o, gates, b_row, b_col, row(p["g_h_a"]), c0, n0, m0b, bsz, mlstm_blk)
    x = _mm_norm_res(hg, p["w_out_a"].astype(BF16), row(p["g_mix_post"][0]), x, tm)
    x = _mlp(x, row(p["g_ffn_pre"][0]), p["w_up"][0].astype(BF16), p["w_down"][0].astype(BF16),
             row(p["g_ffn_post"][0]), tm, FF_TILE)

    kb_w = H_B * 2 * HD_B
    w_kv = p["w_kv"]
    k_new, k_bf, v_new, v_bf = _norm_proj(
        x, row(p["g_kv"]), [w_kv[:, :kb_w].astype(BF16), w_kv[:, kb_w:].astype(BF16)],
        [(1.0, (F32, BF16)), (1.0, (F32, BF16))], tm)
    (qb,) = _norm_proj(x, row(p["g_mix_pre"][1]), [p["w_q_b"].astype(BF16)], [(HD_B ** -0.5, (BF16,))], tm)

    lam_init = _lambda_init(1)
    lam_vec = jnp.stack([p["lambda_q1"], p["lambda_k1"], p["lambda_q2"], p["lambda_k2"]]).astype(F32)
    g_sub = row(p["g_sub_b"])
    shape3 = (bsz, seqlen, kb_w)
    if cache is None:
        att = _attn_prompt(qb.reshape(shape3), k_bf.reshape(shape3), v_bf.reshape(shape3), lam_vec, g_sub,
                           lam_init, ATTN_TILE)
    else:
        ck, cv = cache
        past = ck.shape[1]
        att = _attn_sample(qb.reshape(shape3), ck.reshape(bsz, past, kb_w), cv.reshape(bsz, past, kb_w),
                           k_bf.reshape(shape3), v_bf.reshape(shape3), lam_vec, g_sub, lam_init,
                           min(SAMPLE_KEY_TILE, past))
    x = _mm_norm_res(att.reshape(rows, kb_w), p["w_o_b"].astype(BF16), row(p["g_mix_post"][1]), x, tm)
    x = _mlp(x, row(p["g_ffn_pre"][1]), p["w_up"][1].astype(BF16), p["w_down"][1].astype(BF16),
             row(p["g_ffn_post"][1]), tm, FF_TILE)

    y = x.reshape(bsz, seqlen, d)
    k_out = k_new.reshape(bsz, seqlen, H_B, 2, HD_B)
    v_out = v_new.reshape(bsz, seqlen, H_B, 2 * HD_B)
    return y, (c1[None], n1[None], m1[None, :, :, 0]), k_out, v_out


def kernel(x_prompt, x_sample, cache_k, cache_v, state_c, state_n, state_m, w_in_a, b_i_a, b_f_a, g_h_a, w_out_a,
           g_kv, w_kv, w_q_b, lambda_q1, lambda_k1, lambda_q2, lambda_k2, g_sub_b, w_o_b, g_mix_pre, g_mix_post,
           g_ffn_pre, g_ffn_post, w_up, w_down):
    assert w_in_a.shape[0] == 1 and w_q_b.shape[0] == 1 and g_mix_pre.shape[0] == 2
    dk, dv = state_c.shape[3], state_c.shape[4]
    p = {"dk": dk, "dv": dv,
         "w_in_a": w_in_a[0], "b_i_a": b_i_a[0], "b_f_a": b_f_a[0], "g_h_a": g_h_a[0], "w_out_a": w_out_a[0],
         "g_kv": g_kv, "w_kv": w_kv, "w_q_b": w_q_b[0],
         "lambda_q1": lambda_q1[0], "lambda_k1": lambda_k1[0], "lambda_q2": lambda_q2[0], "lambda_k2": lambda_k2[0],
         "g_sub_b": g_sub_b[0], "w_o_b": w_o_b[0], "g_mix_pre": g_mix_pre, "g_mix_post": g_mix_post,
         "g_ffn_pre": g_ffn_pre, "g_ffn_post": g_ffn_post, "w_up": w_up, "w_down": w_down}
    bsz = x_prompt.shape[0]
    zero_state = (jnp.zeros((bsz, H_A, dk, dv), F32), jnp.zeros((bsz, H_A, dk), F32), jnp.zeros((bsz, H_A), F32))
    y_p, (c_p, n_p, m_p), k_p, v_p = _trunk(x_prompt, zero_state, None, p, MLSTM_BLOCK)
    y_s, (c_s, n_s, m_s), k_s, v_s = _trunk(
        x_sample, (state_c[0].astype(F32), state_n[0].astype(F32), state_m[0].astype(F32)),
        (cache_k, cache_v), p, x_sample.shape[1])
    return (y_p, y_s, c_p, n_p, m_p, k_p, v_p, c_s, n_s, m_s, k_s, v_s)
```

```python
import functools
import math

import jax
import jax.numpy as jnp
from jax import lax
from jax.experimental import pallas as pl
from jax.experimental.pallas import tpu as pltpu

F32 = jnp.float32
BF16 = jnp.bfloat16

EPS = 1e-6
NEG_INF = -1e30
CHUNK = 64
LANES = 128
H_A = 8
H_B = 8
HD_B = 64
MLSTM_BLOCK = 128
MLSTM_STREAMS = 1
ATTN_TILE = 512
KEY_LOOP_UNROLL = 2
ROW_TILE = 512
FF_TILE = 1024
SAMPLE_KEY_TILE = 512
VMEM_LIMIT_BYTES = 48 * 1024 * 1024


def _params(*sem, flags=None):
    return pltpu.CompilerParams(dimension_semantics=sem, vmem_limit_bytes=VMEM_LIMIT_BYTES, flags=flags)


def _rms(x, g):
    return x * lax.rsqrt(jnp.mean(x * x, axis=-1, keepdims=True) + EPS) * g


def _log_sigmoid(x):
    return jnp.minimum(x, 0.0) - jnp.log1p(jnp.exp(-jnp.abs(x)))


def _dot_nt(a, b):
    return lax.dot_general(a, b, (((1,), (1,)), ((), ())), preferred_element_type=F32)


def _dot_tn(a, b):
    return lax.dot_general(a, b, (((0,), (0,)), ((), ())), preferred_element_type=F32)


def _norm_proj_kernel(x_ref, g_ref, *refs, specs):
    n_w = len(specs)
    w_refs, o_refs = refs[:n_w], refs[n_w:]
    xn = _rms(x_ref[...], g_ref[...]).astype(BF16)
    oi = 0
    for w_ref, (scale, dtypes) in zip(w_refs, specs):
        acc = jnp.dot(xn, w_ref[...], preferred_element_type=F32)
        if scale != 1.0:
            acc = acc * scale
        for dt in dtypes:
            o_refs[oi][...] = acc.astype(dt)
            oi += 1


def _norm_proj(x, g, ws, specs, tm):
    m, d = x.shape
    assert m % tm == 0
    out_shapes, out_specs = [], []
    for w, (_, dtypes) in zip(ws, specs):
        for dt in dtypes:
            out_shapes.append(jax.ShapeDtypeStruct((m, w.shape[1]), dt))
            out_specs.append(pl.BlockSpec((tm, w.shape[1]), lambda i: (i, 0)))
    in_specs = [pl.BlockSpec((tm, d), lambda i: (i, 0)), pl.BlockSpec((1, d), lambda i: (0, 0))]
    in_specs += [pl.BlockSpec(w.shape, lambda i: (0, 0)) for w in ws]
    return pl.pallas_call(
        functools.partial(_norm_proj_kernel, specs=tuple(specs)),
        grid=(m // tm,),
        in_specs=in_specs,
        out_specs=out_specs,
        out_shape=out_shapes,
        compiler_params=_params("parallel"),
        name="norm_proj",
    )(x, g, *ws)


def _mm_norm_res_kernel(h_ref, w_ref, g_ref, x_ref, o_ref):
    acc = jnp.dot(h_ref[...], w_ref[...], preferred_element_type=F32)
    o_ref[...] = x_ref[...] + _rms(acc, g_ref[...])


def _mm_norm_res(h, w, g, x, tm):
    m, k = h.shape
    n = w.shape[1]
    assert m % tm == 0
    return pl.pallas_call(
        _mm_norm_res_kernel,
        grid=(m // tm,),
        in_specs=[pl.BlockSpec((tm, k), lambda i: (i, 0)),
                  pl.BlockSpec((k, n), lambda i: (0, 0)),
                  pl.BlockSpec((1, n), lambda i: (0, 0)),
                  pl.BlockSpec((tm, n), lambda i: (i, 0))],
        out_specs=pl.BlockSpec((tm, n), lambda i: (i, 0)),
        out_shape=jax.ShapeDtypeStruct((m, n), F32),
        compiler_params=_params("parallel"),
        name="mm_norm_res",
    )(h, w, g, x)


def _mlp_kernel(x_ref, gpre_ref, wup_ref, wdn_ref, gpost_ref, o_ref, xn_sc, acc_sc):
    j = pl.program_id(1)

    @pl.when(j == 0)
    def _():
        xn_sc[...] = _rms(x_ref[...], gpre_ref[...]).astype(BF16)
        acc_sc[...] = jnp.zeros_like(acc_sc)

    h = jnp.maximum(jnp.dot(xn_sc[...], wup_ref[...], preferred_element_type=F32), 0.0)
    acc_sc[...] += jnp.dot((h * h).astype(BF16), wdn_ref[...], preferred_element_type=F32)

    @pl.when(j == pl.num_programs(1) - 1)
    def _():
        o_ref[...] = x_ref[...] + _rms(acc_sc[...], gpost_ref[...])


def _mlp(x, g_pre, w_up, w_down, g_post, tm, tf):
    m, d = x.shape
    ff = w_up.shape[1]
    assert m % tm == 0 and ff % tf == 0
    return pl.pallas_call(
        _mlp_kernel,
        grid=(m // tm, ff // tf),
        in_specs=[pl.BlockSpec((tm, d), lambda i, j: (i, 0)),
                  pl.BlockSpec((1, d), lambda i, j: (0, 0)),
                  pl.BlockSpec((d, tf), lambda i, j: (0, j)),
                  pl.BlockSpec((tf, d), lambda i, j: (j, 0)),
                  pl.BlockSpec((1, d), lambda i, j: (0, 0))],
        out_specs=pl.BlockSpec((tm, d), lambda i, j: (i, 0)),
        out_shape=jax.ShapeDtypeStruct((m, d), F32),
        scratch_shapes=[pltpu.VMEM((tm, d), BF16), pltpu.VMEM((tm, d), F32)],
        compiler_params=_params("parallel", "arbitrary"),
        name="mlp",
    )(x, g_pre, w_up, w_down, g_post)


def _mlstm_kernel(q_ref, k_ref, v_ref, o_ref, gate_ref, brow_ref, bcol_ref, gh_ref, c0_ref, n0_ref, m0_ref,
                  hg_ref, c_ref, n_ref, m_ref, *, nb, blk, heads, dk, dv):
    t = pl.program_id(1)

    @pl.when(t == 0)
    def _():
        c_ref[...] = c0_ref[...]
        n_ref[...] = n0_ref[...]
        m_ref[...] = m0_ref[...]

    for b in range(nb):
        _mlstm_block(b, q_ref, k_ref, v_ref, o_ref, gate_ref, brow_ref, bcol_ref, gh_ref,
                     hg_ref, c_ref, n_ref, m_ref, blk=blk, heads=heads, dk=dk, dv=dv)


def _mlstm_block(b, q_ref, k_ref, v_ref, o_ref, gate_ref, brow_ref, bcol_ref, gh_ref,
                 hg_ref, c_ref, n_ref, m_ref, *, blk, heads, dk, dv):
    gates = gate_ref[b]
    z_col = gates + brow_ref[...]
    lf_col = _log_sigmoid(z_col)
    row = lax.broadcasted_iota(jnp.int32, (blk, blk), 0)
    col = lax.broadcasted_iota(jnp.int32, (blk, blk), 1)
    causal = row >= col
    tril = causal.astype(F32)
    fcum_col = jnp.dot(tril, lf_col, preferred_element_type=F32, precision=lax.Precision.HIGHEST)

    if blk < LANES:
        gates_sq = jnp.concatenate([gates, jnp.zeros((LANES - blk, LANES), F32)], axis=0)
    else:
        gates_sq = gates
    gates_row = gates_sq.T[:, :blk]
    z_row = gates_row[0:2 * heads, :] + bcol_ref[...]
    li_row = z_row[0:heads, :]
    lf_row = _log_sigmoid(z_row[heads:2 * heads, :])
    triu = (row <= col).astype(F32)
    fcum_row = jnp.dot(lf_row, triu, preferred_element_type=F32, precision=lax.Precision.HIGHEST)
    r_row = li_row - fcum_row

    hr = range(heads)
    qs = [q_ref[b, :, h * dk:(h + 1) * dk] for h in hr]
    ks = [k_ref[b, :, h * dk:(h + 1) * dk] for h in hr]
    vs = [v_ref[b, :, h * dv:(h + 1) * dv] for h in hr]
    c0s = [c_ref[b, h] for h in hr]
    n0s = [n_ref[b, h:h + 1, :] for h in hr]
    m0s = [m_ref[b, h:h + 1, 0:1] for h in hr]
    fcs = [fcum_col[:, heads + h:heads + h + 1] for h in hr]

    qk = [_dot_nt(qs[h], ks[h]) for h in hr]
    qc = [jnp.dot(qs[h], c0s[h].astype(BF16), preferred_element_type=F32) for h in hr]
    qn0 = [jnp.sum(qs[h].astype(F32) * n0s[h], axis=1, keepdims=True) for h in hr]

    dmat = [jnp.where(causal, fcs[h] + r_row[h:h + 1, :], NEG_INF) for h in hr]
    a = [fcs[h] + m0s[h] for h in hr]
    m_new = [jnp.maximum(a[h], jnp.max(dmat[h], axis=1, keepdims=True)) for h in hr]
    s = [qk[h] * jnp.exp(dmat[h] - m_new[h]) for h in hr]
    inter = [jnp.exp(a[h] - m_new[h]) for h in hr]
    sv = [jnp.dot(s[h].astype(BF16), vs[h], preferred_element_type=F32) for h in hr]
    qn = [jnp.sum(s[h], axis=1, keepdims=True) + inter[h] * qn0[h] for h in hr]
    hh = [(sv[h] + inter[h] * qc[h]) / jnp.maximum(jnp.abs(qn[h]), jnp.exp(-m_new[h])) for h in hr]

    for h in hr:
        hn = _rms(hh[h], gh_ref[:, h * dv:(h + 1) * dv])
        og = o_ref[b, :, h * dv:(h + 1) * dv].astype(F32)
        hg_ref[b, :, h * dv:(h + 1) * dv] = (hn * jax.nn.sigmoid(og)).astype(BF16)

    m_last = [m_new[h][blk - 1:blk, :] for h in hr]
    w_last = [jnp.exp(fcs[h][blk - 1:blk, :] - fcs[h] + z_col[:, h:h + 1] - m_last[h]) for h in hr]
    decay = [jnp.exp(a[h][blk - 1:blk, :] - m_last[h]) for h in hr]
    kw = [ks[h].astype(F32) * w_last[h] for h in hr]
    kv = [_dot_tn(kw[h].astype(BF16), vs[h]) for h in hr]
    for h in hr:
        c_ref[b, h] = decay[h] * c0s[h] + kv[h]
        n_ref[b, h:h + 1, :] = decay[h] * n0s[h] + jnp.sum(kw[h], axis=0, keepdims=True)
        m_ref[b, h:h + 1, :] = jnp.broadcast_to(m_last[h], (1, LANES))


def _mlstm(q, k, v, o, gates, b_row, b_col, g_h, c0, n0, m0, blk, nb):
    bsz, seqlen, _ = q.shape
    heads, dk, dv = c0.shape[1], c0.shape[2], c0.shape[3]
    assert seqlen % blk == 0 and bsz % nb == 0
    row_map = lambda b, t: (b, t, 0)
    const2 = lambda b, t: (0, 0)
    st4 = lambda b, t: (b, 0, 0, 0)
    st3 = lambda b, t: (b, 0, 0)
    return pl.pallas_call(
        functools.partial(_mlstm_kernel, nb=nb, blk=blk, heads=heads, dk=dk, dv=dv),
        grid=(bsz // nb, seqlen // blk),
        in_specs=[pl.BlockSpec((nb, blk, heads * dk), row_map),
                  pl.BlockSpec((nb, blk, heads * dk), row_map),
                  pl.BlockSpec((nb, blk, heads * dv), row_map),
                  pl.BlockSpec((nb, blk, heads * dv), row_map),
                  pl.BlockSpec((nb, blk, LANES), row_map),
                  pl.BlockSpec((1, LANES), const2),
                  pl.BlockSpec((2 * heads, 1), const2),
                  pl.BlockSpec((1, heads * dv), const2),
                  pl.BlockSpec((nb, heads, dk, dv), st4),
                  pl.BlockSpec((nb, heads, dk), st3),
                  pl.BlockSpec((nb, heads, LANES), st3)],
        out_specs=[pl.BlockSpec((nb, blk, heads * dv), row_map),
                   pl.BlockSpec((nb, heads, dk, dv), st4),
                   pl.BlockSpec((nb, heads, dk), st3),
                   pl.BlockSpec((nb, heads, LANES), st3)],
        out_shape=[jax.ShapeDtypeStruct((bsz, seqlen, heads * dv), BF16),
                   jax.ShapeDtypeStruct((bsz, heads, dk, dv), F32),
                   jax.ShapeDtypeStruct((bsz, heads, dk), F32),
                   jax.ShapeDtypeStruct((bsz, heads, LANES), F32)],
        compiler_params=_params("parallel", "arbitrary"),
        name="mlstm",
    )(q, k, v, o, gates, b_row, b_col, g_h, c0, n0, m0)


def _stack_maps(q):
    lane = lax.broadcasted_iota(jnp.int32, q.shape, 1)
    zero = jnp.zeros_like(q)
    return jnp.concatenate([jnp.where(lane < HD_B, q, zero), jnp.where(lane >= HD_B, q, zero)], axis=0)


def _lambda(lam_ref, lam_init):
    lv = lam_ref[...]
    a = jnp.sum(lv[0:1] * lv[1:2], axis=1, keepdims=True)
    b = jnp.sum(lv[2:3] * lv[3:4], axis=1, keepdims=True)
    return jnp.exp(a) - jnp.exp(b) + lam_init


def _softmax_step(s, v, m_prev, l_prev, acc_prev):
    m_new = jnp.maximum(m_prev, jnp.max(s, axis=1, keepdims=True))
    alpha = jnp.exp2(m_prev - m_new)
    p = jnp.exp2(s - m_new)
    l_new = alpha * l_prev + jnp.sum(p, axis=1, keepdims=True)
    acc_new = alpha * acc_prev + jnp.dot(p.astype(BF16), v, preferred_element_type=F32)
    return m_new, l_new, acc_new


def _finish_head(acc, l, lam, g_sub, lam_init, tq):
    w = acc / l
    o = w[0:tq] - lam * w[tq:2 * tq]
    return _rms(o, g_sub) * (1.0 - lam_init)


def _attn_prompt_kernel(q_ref, kt_ref, v_ref, lam_ref, gsub_ref, o_ref, qs_sc, m_sc, l_sc, acc_sc, *, tile, lam_init):
    i = pl.program_id(2)
    qs_sc[...] = _stack_maps(q_ref[0])
    m_sc[...] = jnp.full(m_sc.shape, NEG_INF, F32)
    l_sc[...] = jnp.zeros_like(l_sc)
    acc_sc[...] = jnp.zeros_like(acc_sc)
    n_lane_tiles = tile // LANES

    def step(j, n_tiles, masked):
        qs = qs_sc[...]
        ss = [jnp.dot(qs, kt_ref[0, j + t], preferred_element_type=F32) for t in range(n_tiles)]
        if masked:
            row = lax.broadcasted_iota(jnp.int32, ss[0].shape, 0) & (tile - 1)
            col = lax.broadcasted_iota(jnp.int32, ss[0].shape, 1)
            ss = [jnp.where((col // CHUNK) <= (row // CHUNK), s, NEG_INF) for s in ss]
        m_prev = m_sc[...]
        m_new = m_prev
        for s in ss:
            m_new = jnp.maximum(m_new, jnp.max(s, axis=1, keepdims=True))
        alpha = jnp.exp2(m_prev - m_new)
        m_wide = jnp.concatenate([m_new] * n_lane_tiles, axis=1)
        l_new = alpha * l_sc[...]
        acc_new = alpha * acc_sc[...]
        for t, s in enumerate(ss):
            p = jnp.exp2((s - m_wide).astype(BF16))
            p_sum = p[:, 0:LANES]
            for c in range(1, n_lane_tiles):
                p_sum = p_sum + p[:, c * LANES:(c + 1) * LANES]
            l_new = l_new + p_sum.astype(F32)
            vt = v_ref[0, 0, pl.ds(pl.multiple_of((j + t) * tile, tile), tile), :]
            acc_new = acc_new + jnp.dot(p, vt, preferred_element_type=F32)
        l_sc[...] = l_new
        acc_sc[...] = acc_new
        m_sc[...] = m_new

    def body(jj, carry):
        for u in range(KEY_LOOP_UNROLL):
            step(KEY_LOOP_UNROLL * jj + u, 1, False)
        return carry

    n_full = i // KEY_LOOP_UNROLL
    lax.fori_loop(0, n_full, body, 0)
    for rem in range(1, KEY_LOOP_UNROLL):
        @pl.when(i - KEY_LOOP_UNROLL * n_full >= rem)
        def _():
            step(KEY_LOOP_UNROLL * n_full + rem - 1, 1, False)
    step(i, 1, True)
    l = jnp.sum(l_sc[...], axis=1, keepdims=True)
    o = _finish_head(acc_sc[...], l, _lambda(lam_ref, lam_init), gsub_ref[...], lam_init, tile)
    o_ref[0] = o.astype(BF16)


def _attn_prompt(q, kt, v, lam_vec, g_sub, lam_init):
    bsz, seqlen, width = q.shape
    tile = kt.shape[3]
    heads = width // LANES
    assert seqlen % tile == 0 and tile % CHUNK == 0 and (tile & (tile - 1)) == 0 and tile % LANES == 0
    return pl.pallas_call(
        functools.partial(_attn_prompt_kernel, tile=tile, lam_init=lam_init),
        grid=(bsz, heads, seqlen // tile),
        in_specs=[pl.BlockSpec((1, tile, LANES), lambda b, h, i: (b, i, h)),
                  pl.BlockSpec((1, seqlen // tile, LANES, tile), lambda b, h, i: (b, 0, h, 0)),
                  pl.BlockSpec((1, 1, seqlen, LANES), lambda b, h, i: (b, h, 0, 0)),
                  pl.BlockSpec((4, HD_B), lambda b, h, i: (0, 0)),
                  pl.BlockSpec((1, LANES), lambda b, h, i: (0, h))],
        out_specs=pl.BlockSpec((1, tile, LANES), lambda b, h, i: (b, i, h)),
        out_shape=jax.ShapeDtypeStruct((bsz, seqlen, width), BF16),
        scratch_shapes=[pltpu.VMEM((2 * tile, LANES), BF16),
                        pltpu.VMEM((2 * tile, LANES), F32),
                        pltpu.VMEM((2 * tile, LANES), F32),
                        pltpu.VMEM((2 * tile, LANES), F32)],
        compiler_params=_params("parallel", "parallel", "arbitrary"),
        name="attn_prompt",
    )(q, kt, v, lam_vec, g_sub)


def _kv_proj_t_kernel(x_ref, g_ref, wkt_ref, wv_ref, kt_ref, ktb_ref, v_ref, vb_ref, *, heads):
    xn = _rms(x_ref[...], g_ref[...]).astype(BF16)
    kt = _dot_nt(wkt_ref[...], xn)
    kt_ref[0] = kt
    ktb_ref[0, 0] = kt.astype(BF16)
    v = jnp.dot(xn, wv_ref[...], preferred_element_type=F32)
    v_ref[...] = v
    for h in range(heads):
        vb_ref[0, h] = v[:, h * LANES:(h + 1) * LANES].astype(BF16)


def _kv_proj_t(x, g, wkt, wv, bsz, tm):
    rows, d = x.shape
    seqlen = rows // bsz
    width = wv.shape[1]
    heads = width // LANES
    assert seqlen % tm == 0
    nt = seqlen // tm
    return pl.pallas_call(
        functools.partial(_kv_proj_t_kernel, heads=heads),
        grid=(bsz, nt),
        in_specs=[pl.BlockSpec((tm, d), lambda b, i: (b * nt + i, 0)),
                  pl.BlockSpec((1, d), lambda b, i: (0, 0)),
                  pl.BlockSpec((width, d), lambda b, i: (0, 0)),
                  pl.BlockSpec((d, width), lambda b, i: (0, 0))],
        out_specs=[pl.BlockSpec((1, width, tm), lambda b, i: (b, 0, i)),
                   pl.BlockSpec((1, 1, width, tm), lambda b, i: (b, i, 0, 0)),
                   pl.BlockSpec((tm, width), lambda b, i: (b * nt + i, 0)),
                   pl.BlockSpec((1, heads, tm, LANES), lambda b, i: (b, 0, i, 0))],
        out_shape=[jax.ShapeDtypeStruct((bsz, width, seqlen), F32),
                   jax.ShapeDtypeStruct((bsz, nt, width, tm), BF16),
                   jax.ShapeDtypeStruct((rows, width), F32),
                   jax.ShapeDtypeStruct((bsz, heads, seqlen, LANES), BF16)],
        compiler_params=_params("parallel", "parallel"),
        name="kv_proj_t",
    )(x, g, wkt, wv)


def _attn_sample_kernel(q_ref, ck_ref, cv_ref, kn_ref, vn_ref, lam_ref, gsub_ref, o_ref, qs_sc, m_sc, l_sc, acc_sc,
                        *, lq, heads, past, lam_init):
    j = pl.program_id(1)

    @pl.when(j == 0)
    def _():
        for h in range(heads):
            qs_sc[h] = _stack_maps(q_ref[0, :, h * LANES:(h + 1) * LANES])
        m_sc[...] = jnp.full(m_sc.shape, NEG_INF, F32)
        l_sc[...] = jnp.zeros_like(l_sc)
        acc_sc[...] = jnp.zeros_like(acc_sc)

    def update(h, kt, vt, mask):
        s = _dot_nt(qs_sc[h], kt)
        if mask is not None:
            s = jnp.where(mask, s, NEG_INF)
        m_new, l_new, acc_new = _softmax_step(s, vt, m_sc[h], l_sc[h], acc_sc[h])
        m_sc[h] = m_new
        l_sc[h] = l_new
        acc_sc[h] = acc_new

    for h in range(heads):
        update(h, ck_ref[0, :, h * LANES:(h + 1) * LANES].astype(BF16),
               cv_ref[0, :, h * LANES:(h + 1) * LANES].astype(BF16), None)

    @pl.when(j == pl.num_programs(1) - 1)
    def _():
        row = lax.broadcasted_iota(jnp.int32, (2 * lq, lq), 0)
        row = jnp.where(row >= lq, row - lq, row)
        col = lax.broadcasted_iota(jnp.int32, (2 * lq, lq), 1)
        mask = ((past + col) // CHUNK) <= ((past + row) // CHUNK)
        lam = _lambda(lam_ref, lam_init)
        for h in range(heads):
            hs = slice(h * LANES, (h + 1) * LANES)
            update(h, kn_ref[0, :, hs], vn_ref[0, :, hs], mask)
            o = _finish_head(acc_sc[h], l_sc[h], lam, gsub_ref[:, hs], lam_init, lq)
            o_ref[0, :, hs] = o.astype(BF16)


def _attn_sample(q, cache_k, cache_v, k_new, v_new, lam_vec, g_sub, lam_init, tkc):
    bsz, lq, width = q.shape
    past = cache_k.shape[1]
    heads = width // LANES
    assert past % tkc == 0 and past % CHUNK == 0
    return pl.pallas_call(
        functools.partial(_attn_sample_kernel, lq=lq, heads=heads, past=past, lam_init=lam_init),
        grid=(bsz, past // tkc),
        in_specs=[pl.BlockSpec((1, lq, width), lambda b, j: (b, 0, 0)),
                  pl.BlockSpec((1, tkc, width), lambda b, j: (b, j, 0)),
                  pl.BlockSpec((1, tkc, width), lambda b, j: (b, j, 0)),
                  pl.BlockSpec((1, lq, width), lambda b, j: (b, 0, 0)),
                  pl.BlockSpec((1, lq, width), lambda b, j: (b, 0, 0)),
                  pl.BlockSpec((4, HD_B), lambda b, j: (0, 0)),
                  pl.BlockSpec((1, width), lambda b, j: (0, 0))],
        out_specs=pl.BlockSpec((1, lq, width), lambda b, j: (b, 0, 0)),
        out_shape=jax.ShapeDtypeStruct((bsz, lq, width), BF16),
        scratch_shapes=[pltpu.VMEM((heads, 2 * lq, LANES), BF16),
                        pltpu.VMEM((heads, 2 * lq, 1), F32),
                        pltpu.VMEM((heads, 2 * lq, 1), F32),
                        pltpu.VMEM((heads, 2 * lq, LANES), F32)],
        compiler_params=_params("parallel", "arbitrary"),
        name="attn_sample",
    )(q, cache_k, cache_v, k_new, v_new, lam_vec, g_sub)


def _lambda_init(layer_idx):
    return 0.8 - 0.6 * math.exp(-0.3 * layer_idx)


def _trunk(x3, state, cache, p, mlstm_blk):
    bsz, seqlen, d = x3.shape
    rows = bsz * seqlen
    tm = min(ROW_TILE, rows)
    x = x3.reshape(rows, d)
    row = lambda g: g.reshape(1, -1)

    qk_w = H_A * p["dk"]
    v_w = H_A * p["dv"]
    w_in = p["w_in_a"]
    w_parts = [w_in[:, 0:qk_w], w_in[:, qk_w:2 * qk_w], w_in[:, 2 * qk_w:2 * qk_w + v_w],
               w_in[:, 2 * qk_w + v_w:2 * qk_w + 2 * v_w]]
    w_gate = jnp.pad(w_in[:, 2 * qk_w + 2 * v_w:], ((0, 0), (0, LANES - 2 * H_A)))
    ws = [w.astype(BF16) for w in w_parts] + [w_gate.astype(BF16)]
    specs = [(1.0, (BF16,)), (p["dk"] ** -0.5, (BF16,)), (1.0, (BF16,)), (1.0, (BF16,)), (1.0, (F32,))]
    q, k, v, o, gates = _norm_proj(x, row(p["g_mix_pre"][0]), ws, specs, tm)

    bias = jnp.concatenate([p["b_i_a"], p["b_f_a"]]).astype(F32)
    b_row = jnp.pad(bias, (0, LANES - 2 * H_A)).reshape(1, LANES)
    b_col = bias.reshape(2 * H_A, 1)
    c0, n0, m0 = state
    m0b = jnp.broadcast_to(m0[:, :, None], (bsz, H_A, LANES))
    r3 = lambda a: a.reshape(bsz, seqlen, a.shape[-1])
    hg, c1, n1, m1 = _mlstm(r3(q), r3(k), r3(v), r3(o), r3(gates), b_row, b_col, row(p["g_h_a"]), c0, n0, m0b,
                            mlstm_blk, MLSTM_STREAMS)
    x = _mm_norm_res(hg.reshape(rows, v_w), p["w_out_a"].astype(BF16), row(p["g_mix_post"][0]), x, tm)
    x = _mlp(x, row(p["g_ffn_pre"][0]), p["w_up"][0].astype(BF16), p["w_down"][0].astype(BF16),
             row(p["g_ffn_post"][0]), tm, FF_TILE)

    kb_w = H_B * 2 * HD_B
    w_k = p["w_kv"][:, :kb_w]
    w_v = p["w_kv"][:, kb_w:].astype(BF16)
    (qb,) = _norm_proj(x, row(p["g_mix_pre"][1]), [p["w_q_b"].astype(BF16)],
                       [(HD_B ** -0.5 * math.log2(math.e), (BF16,))], tm)

    lam_init = _lambda_init(1)
    lam_vec = jnp.stack([p["lambda_q1"], p["lambda_k1"], p["lambda_q2"], p["lambda_k2"]]).astype(F32)
    g_sub = row(p["g_sub_b"])
    shape3 = (bsz, seqlen, kb_w)
    if cache is None:
        assert ATTN_TILE == tm
        kt_new, kt_bf, v_new, v_bf = _kv_proj_t(x, row(p["g_kv"]), w_k.T.astype(BF16), w_v, bsz, ATTN_TILE)
        k_out = kt_new.reshape(bsz, H_B, 2, HD_B, seqlen).transpose(0, 4, 1, 2, 3)
        att = _attn_prompt(qb.reshape(shape3), kt_bf, v_bf, lam_vec, g_sub, lam_init)
    else:
        k_new, k_bf, v_new, v_bf = _norm_proj(x, row(p["g_kv"]), [w_k.astype(BF16), w_v],
                                              [(1.0, (F32, BF16)), (1.0, (F32, BF16))], tm)
        k_out = k_new.reshape(bsz, seqlen, H_B, 2, HD_B)
        ck, cv = cache
        past = ck.shape[1]
        att = _attn_sample(qb.reshape(shape3), ck.reshape(bsz, past, kb_w), cv.reshape(bsz, past, kb_w),
                           k_bf.reshape(shape3), v_bf.reshape(shape3), lam_vec, g_sub, lam_init,
                           min(SAMPLE_KEY_TILE, past))
    x = _mm_norm_res(att.reshape(rows, kb_w), p["w_o_b"].astype(BF16), row(p["g_mix_post"][1]), x, tm)
    x = _mlp(x, row(p["g_ffn_pre"][1]), p["w_up"][1].astype(BF16), p["w_down"][1].astype(BF16),
             row(p["g_ffn_post"][1]), tm, FF_TILE)

    y = x.reshape(bsz, seqlen, d)
    v_out = v_new.reshape(bsz, seqlen, H_B, 2 * HD_B)
    return y, (c1[None], n1[None], m1[None, :, :, 0]), k_out, v_out


def kernel(x_prompt, x_sample, cache_k, cache_v, state_c, state_n, state_m, w_in_a, b_i_a, b_f_a, g_h_a, w_out_a,
           g_kv, w_kv, w_q_b, lambda_q1, lambda_k1, lambda_q2, lambda_k2, g_sub_b, w_o_b, g_mix_pre, g_mix_post,
           g_ffn_pre, g_ffn_post, w_up, w_down):
    assert w_in_a.shape[0] == 1 and w_q_b.shape[0] == 1 and g_mix_pre.shape[0] == 2
    dk, dv = state_c.shape[3], state_c.shape[4]
    p = {"dk": dk, "dv": dv,
         "w_in_a": w_in_a[0], "b_i_a": b_i_a[0], "b_f_a": b_f_a[0], "g_h_a": g_h_a[0], "w_out_a": w_out_a[0],
         "g_kv": g_kv, "w_kv": w_kv, "w_q_b": w_q_b[0],
         "lambda_q1": lambda_q1[0], "lambda_k1": lambda_k1[0], "lambda_q2": lambda_q2[0], "lambda_k2": lambda_k2[0],
         "g_sub_b": g_sub_b[0], "w_o_b": w_o_b[0], "g_mix_pre": g_mix_pre, "g_mix_post": g_mix_post,
         "g_ffn_pre": g_ffn_pre, "g_ffn_post": g_ffn_post, "w_up": w_up, "w_down": w_down}
    bsz = x_prompt.shape[0]
    zero_state = (jnp.zeros((bsz, H_A, dk, dv), F32), jnp.zeros((bsz, H_A, dk), F32), jnp.zeros((bsz, H_A), F32))
    y_p, (c_p, n_p, m_p), k_p, v_p = _trunk(x_prompt, zero_state, None, p, MLSTM_BLOCK)
    y_s, (c_s, n_s, m_s), k_s, v_s = _trunk(
        x_sample, (state_c[0].astype(F32), state_n[0].astype(F32), state_m[0].astype(F32)),
        (cache_k, cache_v), p, x_sample.shape[1])
    return (y_p, y_s, c_p, n_p, m_p, k_p, v_p, c_s, n_s, m_s, k_s, v_s)
```

```python
import functools
import math

import jax
import jax.numpy as jnp
from jax import lax
from jax.experimental import pallas as pl
from jax.experimental.pallas import tpu as pltpu

F32 = jnp.float32
BF16 = jnp.bfloat16

EPS = 1e-6
NEG_INF = -1e30
CHUNK = 64
LANES = 128
H_A = 8
H_B = 8
HD_B = 64
MLSTM_BLOCK = 128
MLSTM_STREAMS = 1
ATTN_TILE = 512
KEY_LOOP_UNROLL = 2
ROW_TILE = 512
FF_TILE = 1024
SAMPLE_KEY_TILE = 512
VMEM_LIMIT_BYTES = 48 * 1024 * 1024


def _params(*sem, flags=None):
    return pltpu.CompilerParams(dimension_semantics=sem, vmem_limit_bytes=VMEM_LIMIT_BYTES, flags=flags)


def _rms(x, g):
    return x * lax.rsqrt(jnp.mean(x * x, axis=-1, keepdims=True) + EPS) * g


def _log_sigmoid(x):
    return jnp.minimum(x, 0.0) - jnp.log1p(jnp.exp(-jnp.abs(x)))


def _dot_nt(a, b):
    return lax.dot_general(a, b, (((1,), (1,)), ((), ())), preferred_element_type=F32)


def _dot_tn(a, b):
    return lax.dot_general(a, b, (((0,), (0,)), ((), ())), preferred_element_type=F32)


def _norm_proj_kernel(x_ref, g_ref, *refs, specs):
    n_w = len(specs)
    w_refs, o_refs = refs[:n_w], refs[n_w:]
    xn = _rms(x_ref[...], g_ref[...]).astype(BF16)
    oi = 0
    for w_ref, (scale, dtypes) in zip(w_refs, specs):
        acc = jnp.dot(xn, w_ref[...], preferred_element_type=F32)
        if scale != 1.0:
            acc = acc * scale
        for dt in dtypes:
            o_refs[oi][...] = acc.astype(dt)
            oi += 1


def _norm_proj(x, g, ws, specs, tm):
    m, d = x.shape
    assert m % tm == 0
    out_shapes, out_specs = [], []
    for w, (_, dtypes) in zip(ws, specs):
        for dt in dtypes:
            out_shapes.append(jax.ShapeDtypeStruct((m, w.shape[1]), dt))
            out_specs.append(pl.BlockSpec((tm, w.shape[1]), lambda i: (i, 0)))
    in_specs = [pl.BlockSpec((tm, d), lambda i: (i, 0)), pl.BlockSpec((1, d), lambda i: (0, 0))]
    in_specs += [pl.BlockSpec(w.shape, lambda i: (0, 0)) for w in ws]
    return pl.pallas_call(
        functools.partial(_norm_proj_kernel, specs=tuple(specs)),
        grid=(m // tm,),
        in_specs=in_specs,
        out_specs=out_specs,
        out_shape=out_shapes,
        compiler_params=_params("parallel"),
        name="norm_proj",
    )(x, g, *ws)


def _mix_mlp_kernel(h_ref, x_ref, wo_ref, gmix_ref, gpre_ref, wup_ref, wdn_ref, gpost_ref, o_ref, *, tf):
    x1 = x_ref[...] + _rms(jnp.dot(h_ref[...], wo_ref[...], preferred_element_type=F32), gmix_ref[...])
    xn = _rms(x1, gpre_ref[...]).astype(BF16)
    acc = None
    for c in range(wup_ref.shape[1] // tf):
        up = jnp.maximum(jnp.dot(xn, wup_ref[:, c * tf:(c + 1) * tf], preferred_element_type=F32), 0.0)
        part = jnp.dot((up * up).astype(BF16), wdn_ref[c * tf:(c + 1) * tf, :], preferred_element_type=F32)
        acc = part if acc is None else acc + part
    o_ref[...] = x1 + _rms(acc, gpost_ref[...])


def _mix_mlp(h, x, w_o, g_mix, g_pre, w_up, w_down, g_post, tm, tf):
    m, d = x.shape
    k = h.shape[1]
    ff = w_up.shape[1]
    assert m % tm == 0 and ff % tf == 0
    resident = lambda shape: pl.BlockSpec(shape, lambda i: (0, 0), pipeline_mode=pl.Buffered(1))
    return pl.pallas_call(
        functools.partial(_mix_mlp_kernel, tf=tf),
        grid=(m // tm,),
        in_specs=[pl.BlockSpec((tm, k), lambda i: (i, 0)),
                  pl.BlockSpec((tm, d), lambda i: (i, 0)),
                  resident((k, d)),
                  resident((1, d)),
                  resident((1, d)),
                  resident((d, ff)),
                  resident((ff, d)),
                  resident((1, d))],
        out_specs=pl.BlockSpec((tm, d), lambda i: (i, 0)),
        out_shape=jax.ShapeDtypeStruct((m, d), F32),
        compiler_params=_params("parallel"),
        name="mix_mlp",
    )(h, x, w_o, g_mix, g_pre, w_up, w_down, g_post)


def _mlstm_kernel(q_ref, k_ref, v_ref, o_ref, gate_ref, brow_ref, bcol_ref, gh_ref, c0_ref, n0_ref, m0_ref,
                  hg_ref, c_ref, n_ref, m_ref, *, nb, blk, heads, dk, dv):
    t = pl.program_id(1)

    @pl.when(t == 0)
    def _():
        c_ref[...] = c0_ref[...]
        n_ref[...] = n0_ref[...]
        m_ref[...] = m0_ref[...]

    for b in range(nb):
        _mlstm_block(b, q_ref, k_ref, v_ref, o_ref, gate_ref, brow_ref, bcol_ref, gh_ref,
                     hg_ref, c_ref, n_ref, m_ref, blk=blk, heads=heads, dk=dk, dv=dv)


def _mlstm_block(b, q_ref, k_ref, v_ref, o_ref, gate_ref, brow_ref, bcol_ref, gh_ref,
                 hg_ref, c_ref, n_ref, m_ref, *, blk, heads, dk, dv):
    gates = gate_ref[b]
    z_col = gates + brow_ref[...]
    lf_col = _log_sigmoid(z_col)
    row = lax.broadcasted_iota(jnp.int32, (blk, blk), 0)
    col = lax.broadcasted_iota(jnp.int32, (blk, blk), 1)
    causal = row >= col
    tril = causal.astype(F32)
    fcum_col = jnp.dot(tril, lf_col, preferred_element_type=F32, precision=lax.Precision.HIGHEST)

    if blk < LANES:
        gates_sq = jnp.concatenate([gates, jnp.zeros((LANES - blk, LANES), F32)], axis=0)
    else:
        gates_sq = gates
    gates_row = gates_sq.T[:, :blk]
    z_row = gates_row[0:2 * heads, :] + bcol_ref[...]
    li_row = z_row[0:heads, :]
    lf_row = _log_sigmoid(z_row[heads:2 * heads, :])
    triu = (row <= col).astype(F32)
    fcum_row = jnp.dot(lf_row, triu, preferred_element_type=F32, precision=lax.Precision.HIGHEST)
    r_row = li_row - fcum_row

    hr = range(heads)
    qs = [q_ref[b, :, h * dk:(h + 1) * dk] for h in hr]
    ks = [k_ref[b, :, h * dk:(h + 1) * dk] for h in hr]
    vs = [v_ref[b, :, h * dv:(h + 1) * dv] for h in hr]
    c0s = [c_ref[b, h] for h in hr]
    n0s = [n_ref[b, h] for h in hr]
    m0s = [m_ref[b, h:h + 1, :] for h in hr]
    fcs = [jnp.broadcast_to(fcum_col[:, heads + h:heads + h + 1], (blk, LANES)) for h in hr]
    lis = [jnp.broadcast_to(z_col[:, h:h + 1], (blk, LANES)) for h in hr]
    ones = jnp.ones((blk, LANES), BF16)
    v1s = [jnp.concatenate([vs[h], ones], axis=1) for h in hr]
    cns = [jnp.concatenate([c0s[h], n0s[h]], axis=1).astype(BF16) for h in hr]

    qk = [_dot_nt(qs[h], ks[h]) for h in hr]
    qcn = [jnp.dot(qs[h], cns[h], preferred_element_type=F32) for h in hr]

    dmat = [jnp.where(causal, fcs[h][:, :blk] + r_row[h:h + 1, :], NEG_INF) for h in hr]
    a = [fcs[h] + m0s[h] for h in hr]
    m_new = [jnp.maximum(a[h], jnp.max(dmat[h], axis=1, keepdims=True)) for h in hr]
    s = [qk[h] * jnp.exp(dmat[h] - m_new[h][:, :blk]) for h in hr]
    inter = [jnp.exp(a[h] - m_new[h]) for h in hr]
    sv1 = [jnp.dot(s[h].astype(BF16), v1s[h], preferred_element_type=F32) for h in hr]
    qn = [sv1[h][:, dv:] + inter[h] * qcn[h][:, dv:] for h in hr]
    hh = [(sv1[h][:, :dv] + inter[h] * qcn[h][:, :dv]) / jnp.maximum(jnp.abs(qn[h]), jnp.exp(-m_new[h]))
          for h in hr]

    for h in hr:
        hn = _rms(hh[h], gh_ref[:, h * dv:(h + 1) * dv])
        og = o_ref[b, :, h * dv:(h + 1) * dv].astype(F32)
        hg_ref[b, :, h * dv:(h + 1) * dv] = (hn * jax.nn.sigmoid(og)).astype(BF16)

    m_last = [m_new[h][blk - 1:blk, :] for h in hr]
    w_last = [jnp.exp(fcs[h][blk - 1:blk, :] - fcs[h] + lis[h] - m_last[h]) for h in hr]
    decay = [jnp.exp(a[h][blk - 1:blk, :] - m_last[h]) for h in hr]
    kw = [ks[h].astype(F32) * w_last[h][:, :dk] for h in hr]
    kv1 = [_dot_tn(kw[h].astype(BF16), v1s[h]) for h in hr]
    for h in hr:
        c_ref[b, h] = decay[h] * c0s[h] + kv1[h][:, :dv]
        n_ref[b, h] = decay[h] * n0s[h] + kv1[h][:, dv:]
        m_ref[b, h:h + 1, :] = m_last[h]


def _mlstm(q, k, v, o, gates, b_row, b_col, g_h, c0, n0, m0, blk, nb):
    bsz, seqlen, _ = q.shape
    heads, dk, dv = c0.shape[1], c0.shape[2], c0.shape[3]
    assert seqlen % blk == 0 and bsz % nb == 0 and dv == LANES
    row_map = lambda b, t: (b, t, 0)
    const2 = lambda b, t: (0, 0)
    st4 = lambda b, t: (b, 0, 0, 0)
    st3 = lambda b, t: (b, 0, 0)
    return pl.pallas_call(
        functools.partial(_mlstm_kernel, nb=nb, blk=blk, heads=heads, dk=dk, dv=dv),
        grid=(bsz // nb, seqlen // blk),
        in_specs=[pl.BlockSpec((nb, blk, heads * dk), row_map),
                  pl.BlockSpec((nb, blk, heads * dk), row_map),
                  pl.BlockSpec((nb, blk, heads * dv), row_map),
                  pl.BlockSpec((nb, blk, heads * dv), row_map),
                  pl.BlockSpec((nb, blk, LANES), row_map),
                  pl.BlockSpec((1, LANES), const2),
                  pl.BlockSpec((2 * heads, 1), const2),
                  pl.BlockSpec((1, heads * dv), const2),
                  pl.BlockSpec((nb, heads, dk, dv), st4),
                  pl.BlockSpec((nb, heads, dk, LANES), st4),
                  pl.BlockSpec((nb, heads, LANES), st3)],
        out_specs=[pl.BlockSpec((nb, blk, heads * dv), row_map),
                   pl.BlockSpec((nb, heads, dk, dv), st4),
                   pl.BlockSpec((nb, heads, dk, LANES), st4),
                   pl.BlockSpec((nb, heads, LANES), st3)],
        out_shape=[jax.ShapeDtypeStruct((bsz, seqlen, heads * dv), BF16),
                   jax.ShapeDtypeStruct((bsz, heads, dk, dv), F32),
                   jax.ShapeDtypeStruct((bsz, heads, dk, LANES), F32),
                   jax.ShapeDtypeStruct((bsz, heads, LANES), F32)],
        compiler_params=_params("parallel", "arbitrary"),
        name="mlstm",
    )(q, k, v, o, gates, b_row, b_col, g_h, c0, n0, m0)


def _stack_maps(q):
    lane = lax.broadcasted_iota(jnp.int32, q.shape, 1)
    zero = jnp.zeros_like(q)
    return jnp.concatenate([jnp.where(lane < HD_B, q, zero), jnp.where(lane >= HD_B, q, zero)], axis=0)


def _lambda(lam_ref, lam_init):
    lv = lam_ref[...]
    a = jnp.sum(lv[0:1] * lv[1:2], axis=1, keepdims=True)
    b = jnp.sum(lv[2:3] * lv[3:4], axis=1, keepdims=True)
    return jnp.exp(a) - jnp.exp(b) + lam_init


def _finish_head(acc, l, lam, g_sub, lam_init, tq):
    w = acc / l
    o = w[0:tq] - lam * w[tq:2 * tq]
    return _rms(o, g_sub) * (1.0 - lam_init)


def _attn_prompt_kernel(q_ref, kt_ref, v_ref, lam_ref, gsub_ref, o_ref, qs_sc, m_sc, l_sc, acc_sc, *, tile, lam_init):
    i = pl.program_id(2)
    qs_sc[...] = _stack_maps(q_ref[0])
    m_sc[...] = jnp.full(m_sc.shape, NEG_INF, F32)
    l_sc[...] = jnp.zeros_like(l_sc)
    acc_sc[...] = jnp.zeros_like(acc_sc)
    n_lane_tiles = tile // LANES

    def step(j, n_tiles, masked):
        qs = qs_sc[...]
        ss = [jnp.dot(qs, kt_ref[0, j + t], preferred_element_type=F32) for t in range(n_tiles)]
        if masked:
            row = lax.broadcasted_iota(jnp.int32, ss[0].shape, 0) & (tile - 1)
            col = lax.broadcasted_iota(jnp.int32, ss[0].shape, 1)
            ss = [jnp.where((col // CHUNK) <= (row // CHUNK), s, NEG_INF) for s in ss]
        m_prev = m_sc[...]
        m_new = m_prev
        for s in ss:
            m_new = jnp.maximum(m_new, jnp.max(s, axis=1, keepdims=True))
        alpha = jnp.exp2(m_prev - m_new)
        m_wide = jnp.concatenate([m_new] * n_lane_tiles, axis=1)
        l_new = alpha * l_sc[...]
        acc_new = alpha * acc_sc[...]
        for t, s in enumerate(ss):
            p = jnp.exp2((s - m_wide).astype(BF16))
            p_sum = p[:, 0:LANES]
            for c in range(1, n_lane_tiles):
                p_sum = p_sum + p[:, c * LANES:(c + 1) * LANES]
            l_new = l_new + p_sum.astype(F32)
            vt = v_ref[0, 0, pl.ds(pl.multiple_of((j + t) * tile, tile), tile), :]
            acc_new = acc_new + jnp.dot(p, vt, preferred_element_type=F32)
        l_sc[...] = l_new
        acc_sc[...] = acc_new
        m_sc[...] = m_new

    def body(jj, carry):
        for u in range(KEY_LOOP_UNROLL):
            step(KEY_LOOP_UNROLL * jj + u, 1, False)
        return carry

    n_full = i // KEY_LOOP_UNROLL
    lax.fori_loop(0, n_full, body, 0)
    for rem in range(1, KEY_LOOP_UNROLL):
        @pl.when(i - KEY_LOOP_UNROLL * n_full >= rem)
        def _():
            step(KEY_LOOP_UNROLL * n_full + rem - 1, 1, False)
    step(i, 1, True)
    l = jnp.sum(l_sc[...], axis=1, keepdims=True)
    o = _finish_head(acc_sc[...], l, _lambda(lam_ref, lam_init), gsub_ref[...], lam_init, tile)
    o_ref[0] = o.astype(BF16)


def _attn_prompt(q, kt, v, lam_vec, g_sub, lam_init):
    bsz, seqlen, width = q.shape
    tile = kt.shape[3]
    heads = width // LANES
    assert seqlen % tile == 0 and tile % CHUNK == 0 and (tile & (tile - 1)) == 0 and tile % LANES == 0
    return pl.pallas_call(
        functools.partial(_attn_prompt_kernel, tile=tile, lam_init=lam_init),
        grid=(bsz, heads, seqlen // tile),
        in_specs=[pl.BlockSpec((1, tile, LANES), lambda b, h, i: (b, i, h)),
                  pl.BlockSpec((1, seqlen // tile, LANES, tile), lambda b, h, i: (b, 0, h, 0)),
                  pl.BlockSpec((1, 1, seqlen, LANES), lambda b, h, i: (b, h, 0, 0)),
                  pl.BlockSpec((4, HD_B), lambda b, h, i: (0, 0)),
                  pl.BlockSpec((1, LANES), lambda b, h, i: (0, h))],
        out_specs=pl.BlockSpec((1, tile, LANES), lambda b, h, i: (b, i, h)),
        out_shape=jax.ShapeDtypeStruct((bsz, seqlen, width), BF16),
        scratch_shapes=[pltpu.VMEM((2 * tile, LANES), BF16),
                        pltpu.VMEM((2 * tile, LANES), F32),
                        pltpu.VMEM((2 * tile, LANES), F32),
                        pltpu.VMEM((2 * tile, LANES), F32)],
        compiler_params=_params("parallel", "parallel", "arbitrary"),
        name="attn_prompt",
    )(q, kt, v, lam_vec, g_sub)


def _kv_proj_t_kernel(x_ref, g_ref, gq_ref, wkt_ref, wv_ref, wq_ref, kt_ref, ktb_ref, v_ref, vb_ref, q_ref,
                      *, heads, q_scale):
    x = x_ref[...]
    xr = x * lax.rsqrt(jnp.mean(x * x, axis=-1, keepdims=True) + EPS)
    q = jnp.dot((xr * gq_ref[...]).astype(BF16), wq_ref[...], preferred_element_type=F32)
    q_ref[...] = (q * q_scale).astype(BF16)
    xn = (xr * g_ref[...]).astype(BF16)
    kt = _dot_nt(wkt_ref[...], xn)
    kt_ref[0] = kt
    ktb_ref[0, 0] = kt.astype(BF16)
    v = jnp.dot(xn, wv_ref[...], preferred_element_type=F32)
    v_ref[...] = v
    for h in range(heads):
        vb_ref[0, h] = v[:, h * LANES:(h + 1) * LANES].astype(BF16)


def _kv_proj_t(x, g_kv, g_q, wkt, wv, wq, q_scale, bsz, tm):
    rows, d = x.shape
    seqlen = rows // bsz
    width = wv.shape[1]
    heads = width // LANES
    assert seqlen % tm == 0
    nt = seqlen // tm
    row_blk = lambda b, i: (b * nt + i, 0)
    const = lambda b, i: (0, 0)
    return pl.pallas_call(
        functools.partial(_kv_proj_t_kernel, heads=heads, q_scale=q_scale),
        grid=(bsz, nt),
        in_specs=[pl.BlockSpec((tm, d), row_blk),
                  pl.BlockSpec((1, d), const),
                  pl.BlockSpec((1, d), const),
                  pl.BlockSpec((width, d), const),
                  pl.BlockSpec((d, width), const),
                  pl.BlockSpec((d, width), const)],
        out_specs=[pl.BlockSpec((1, width, tm), lambda b, i: (b, 0, i)),
                   pl.BlockSpec((1, 1, width, tm), lambda b, i: (b, i, 0, 0)),
                   pl.BlockSpec((tm, width), row_blk),
                   pl.BlockSpec((1, heads, tm, LANES), lambda b, i: (b, 0, i, 0)),
                   pl.BlockSpec((tm, width), row_blk)],
        out_shape=[jax.ShapeDtypeStruct((bsz, width, seqlen), F32),
                   jax.ShapeDtypeStruct((bsz, nt, width, tm), BF16),
                   jax.ShapeDtypeStruct((rows, width), F32),
                   jax.ShapeDtypeStruct((bsz, heads, seqlen, LANES), BF16),
                   jax.ShapeDtypeStruct((rows, width), BF16)],
        compiler_params=_params("parallel", "parallel"),
        name="kv_proj_t",
    )(x, g_kv, g_q, wkt, wv, wq)


def _attn_sample_kernel(q_ref, ckt_ref, cv_ref, kn_ref, vn_ref, lam_ref, gsub_ref, o_ref, qs_sc, m_sc, l_sc, acc_sc,
                        *, lq, heads, past, lam_init):
    j = pl.program_id(1)

    @pl.when(j == 0)
    def _():
        for h in range(heads):
            qs_sc[h] = _stack_maps(q_ref[0, :, h * LANES:(h + 1) * LANES])
        m_sc[...] = jnp.full(m_sc.shape, NEG_INF, F32)
        l_sc[...] = jnp.zeros_like(l_sc)
        acc_sc[...] = jnp.zeros_like(acc_sc)

    def update(h, s, vt):
        m_prev = m_sc[h]
        m_new = jnp.maximum(m_prev, jnp.max(s, axis=1, keepdims=True))
        alpha = jnp.exp2(m_prev - m_new)
        p = jnp.exp2(s - m_new)
        l_sc[h] = alpha * l_sc[h] + jnp.sum(p, axis=1, keepdims=True)
        acc_sc[h] = alpha * acc_sc[h] + jnp.dot(p.astype(BF16), vt, preferred_element_type=F32)
        m_sc[h] = m_new

    for h in range(heads):
        kt = ckt_ref[0, h * LANES:(h + 1) * LANES, :].astype(BF16)
        update(h, jnp.dot(qs_sc[h], kt, preferred_element_type=F32), cv_ref[0, :, h * LANES:(h + 1) * LANES])

    @pl.when(j == pl.num_programs(1) - 1)
    def _():
        row = lax.broadcasted_iota(jnp.int32, (2 * lq, lq), 0)
        row = jnp.where(row >= lq, row - lq, row)
        col = lax.broadcasted_iota(jnp.int32, (2 * lq, lq), 1)
        mask = ((past + col) // CHUNK) <= ((past + row) // CHUNK)
        lam = _lambda(lam_ref, lam_init)
        for h in range(heads):
            hs = slice(h * LANES, (h + 1) * LANES)
            update(h, jnp.where(mask, _dot_nt(qs_sc[h], kn_ref[0, :, hs]), NEG_INF), vn_ref[0, :, hs])
            o = _finish_head(acc_sc[h], l_sc[h], lam, gsub_ref[:, hs], lam_init, lq)
            o_ref[0, :, hs] = o.astype(BF16)


def _attn_sample(q, cache_kt, cache_v, k_new, v_new, lam_vec, g_sub, lam_init, tkc):
    bsz, lq, width = q.shape
    past = cache_kt.shape[2]
    heads = width // LANES
    assert past % tkc == 0 and past % CHUNK == 0
    new_blk = pl.BlockSpec((1, lq, width), lambda b, j: (b, 0, 0))
    return pl.pallas_call(
        functools.partial(_attn_sample_kernel, lq=lq, heads=heads, past=past, lam_init=lam_init),
        grid=(bsz, past // tkc),
        in_specs=[new_blk,
                  pl.BlockSpec((1, width, tkc), lambda b, j: (b, 0, j)),
                  pl.BlockSpec((1, tkc, width), lambda b, j: (b, j, 0)),
                  new_blk,
                  new_blk,
                  pl.BlockSpec((4, HD_B), lambda b, j: (0, 0)),
                  pl.BlockSpec((1, width), lambda b, j: (0, 0))],
        out_specs=new_blk,
        out_shape=jax.ShapeDtypeStruct((bsz, lq, width), BF16),
        scratch_shapes=[pltpu.VMEM((heads, 2 * lq, LANES), BF16),
                        pltpu.VMEM((heads, 2 * lq, 1), F32),
                        pltpu.VMEM((heads, 2 * lq, 1), F32),
                        pltpu.VMEM((heads, 2 * lq, LANES), F32)],
        compiler_params=_params("parallel", "arbitrary"),
        name="attn_sample",
    )(q, cache_kt, cache_v, k_new, v_new, lam_vec, g_sub)


def _lambda_init(layer_idx):
    return 0.8 - 0.6 * math.exp(-0.3 * layer_idx)


def _trunk(x3, state, cache, p, mlstm_blk):
    bsz, seqlen, d = x3.shape
    rows = bsz * seqlen
    tm = min(ROW_TILE, rows)
    x = x3.reshape(rows, d)
    row = lambda g: g.reshape(1, -1)

    qk_w = H_A * p["dk"]
    v_w = H_A * p["dv"]
    w_in = p["w_in_a"]
    w_parts = [w_in[:, 0:qk_w], w_in[:, qk_w:2 * qk_w], w_in[:, 2 * qk_w:2 * qk_w + v_w],
               w_in[:, 2 * qk_w + v_w:2 * qk_w + 2 * v_w]]
    w_gate = jnp.pad(w_in[:, 2 * qk_w + 2 * v_w:], ((0, 0), (0, LANES - 2 * H_A)))
    ws = [w.astype(BF16) for w in w_parts] + [w_gate.astype(BF16)]
    specs = [(1.0, (BF16,)), (p["dk"] ** -0.5, (BF16,)), (1.0, (BF16,)), (1.0, (BF16,)), (1.0, (F32,))]
    q, k, v, o, gates = _norm_proj(x, row(p["g_mix_pre"][0]), ws, specs, tm)

    bias = jnp.concatenate([p["b_i_a"], p["b_f_a"]]).astype(F32)
    b_row = jnp.pad(bias, (0, LANES - 2 * H_A)).reshape(1, LANES)
    b_col = bias.reshape(2 * H_A, 1)
    c0, n0, m0 = state
    m0b = jnp.broadcast_to(m0[:, :, None], (bsz, H_A, LANES))
    n0b = jnp.broadcast_to(n0[:, :, :, None], n0.shape + (LANES,))
    r3 = lambda a: a.reshape(bsz, seqlen, a.shape[-1])
    hg, c1, n1b, m1 = _mlstm(r3(q), r3(k), r3(v), r3(o), r3(gates), b_row, b_col, row(p["g_h_a"]), c0, n0b, m0b,
                             mlstm_blk, MLSTM_STREAMS)
    n1 = n1b[:, :, :, 0]
    x = _mix_mlp(hg.reshape(rows, v_w), x, p["w_out_a"].astype(BF16), row(p["g_mix_post"][0]),
                 row(p["g_ffn_pre"][0]), p["w_up"][0].astype(BF16), p["w_down"][0].astype(BF16),
                 row(p["g_ffn_post"][0]), tm, FF_TILE)

    kb_w = H_B * 2 * HD_B
    w_k = p["w_kv"][:, :kb_w]
    w_v = p["w_kv"][:, kb_w:].astype(BF16)
    q_scale = HD_B ** -0.5 * math.log2(math.e)
    w_q = p["w_q_b"].astype(BF16)
    g_q = row(p["g_mix_pre"][1])

    lam_init = _lambda_init(1)
    lam_vec = jnp.stack([p["lambda_q1"], p["lambda_k1"], p["lambda_q2"], p["lambda_k2"]]).astype(F32)
    g_sub = row(p["g_sub_b"])
    shape3 = (bsz, seqlen, kb_w)
    if cache is None:
        assert ATTN_TILE == tm
        kt_new, kt_bf, v_new, v_bf, qb = _kv_proj_t(x, row(p["g_kv"]), g_q, w_k.T.astype(BF16), w_v, w_q, q_scale,
                                                    bsz, ATTN_TILE)
        k_out = kt_new.reshape(bsz, H_B, 2, HD_B, seqlen).transpose(0, 4, 1, 2, 3)
        att = _attn_prompt(qb.reshape(shape3), kt_bf, v_bf, lam_vec, g_sub, lam_init)
    else:
        (qb,) = _norm_proj(x, g_q, [w_q], [(q_scale, (BF16,))], tm)
        k_new, k_bf, v_new, v_bf = _norm_proj(x, row(p["g_kv"]), [w_k.astype(BF16), w_v],
                                              [(1.0, (F32, BF16)), (1.0, (F32, BF16))], tm)
        k_out = k_new.reshape(bsz, seqlen, H_B, 2, HD_B)
        ck, cv = cache
        past = ck.shape[1]
        ckt = ck.transpose(0, 2, 3, 4, 1).reshape(bsz, kb_w, past)
        cvb = cv.reshape(bsz, past, kb_w).astype(BF16)
        att = _attn_sample(qb.reshape(shape3), ckt, cvb, k_bf.reshape(shape3), v_bf.reshape(shape3),
                           lam_vec, g_sub, lam_init, min(SAMPLE_KEY_TILE, past))
    x = _mix_mlp(att.reshape(rows, kb_w), x, p["w_o_b"].astype(BF16), row(p["g_mix_post"][1]),
                 row(p["g_ffn_pre"][1]), p["w_up"][1].astype(BF16), p["w_down"][1].astype(BF16),
                 row(p["g_ffn_post"][1]), tm, FF_TILE)

    y = x.reshape(bsz, seqlen, d)
    v_out = v_new.reshape(bsz, seqlen, H_B, 2 * HD_B)
    return y, (c1[None], n1[None], m1[None, :, :, 0]), k_out, v_out


def kernel(x_prompt, x_sample, cache_k, cache_v, state_c, state_n, state_m, w_in_a, b_i_a, b_f_a, g_h_a, w_out_a,
           g_kv, w_kv, w_q_b, lambda_q1, lambda_k1, lambda_q2, lambda_k2, g_sub_b, w_o_b, g_mix_pre, g_mix_post,
           g_ffn_pre, g_ffn_post, w_up, w_down):
    assert w_in_a.shape[0] == 1 and w_q_b.shape[0] == 1 and g_mix_pre.shape[0] == 2
    dk, dv = state_c.shape[3], state_c.shape[4]
    p = {"dk": dk, "dv": dv,
         "w_in_a": w_in_a[0], "b_i_a": b_i_a[0], "b_f_a": b_f_a[0], "g_h_a": g_h_a[0], "w_out_a": w_out_a[0],
         "g_kv": g_kv, "w_kv": w_kv, "w_q_b": w_q_b[0],
         "lambda_q1": lambda_q1[0], "lambda_k1": lambda_k1[0], "lambda_q2": lambda_q2[0], "lambda_k2": lambda_k2[0],
         "g_sub_b": g_sub_b[0], "w_o_b": w_o_b[0], "g_mix_pre": g_mix_pre, "g_mix_post": g_mix_post,
         "g_ffn_pre": g_ffn_pre, "g_ffn_post": g_ffn_post, "w_up": w_up, "w_down": w_down}
    bsz = x_prompt.shape[0]
    zero_state = (jnp.zeros((bsz, H_A, dk, dv), F32), jnp.zeros((bsz, H_A, dk), F32), jnp.zeros((bsz, H_A), F32))
    y_p, (c_p, n_p, m_p), k_p, v_p = _trunk(x_prompt, zero_state, None, p, MLSTM_BLOCK)
    y_s, (c_s, n_s, m_s), k_s, v_s = _trunk(
        x_sample, (state_c[0].astype(F32), state_n[0].astype(F32), state_m[0].astype(F32)),
        (cache_k, cache_v), p, x_sample.shape[1])
    return (y_p, y_s, c_p, n_p, m_p, k_p, v_p, c_s, n_s, m_s, k_s, v_s)
```

```python
import functools
import math

import jax
import jax.numpy as jnp
from jax import lax
from jax.experimental import pallas as pl
from jax.experimental.pallas import tpu as pltpu

F32 = jnp.float32
BF16 = jnp.bfloat16

EPS = 1e-6
NEG_INF = -1e30
CHUNK = 64
LANES = 128
H_A = 8
H_B = 8
HD_B = 64
MLSTM_BLOCK = 128
MLSTM_STREAMS = 1
ATTN_TILE = 512
KEY_LOOP_UNROLL = 2
ATTN_HEAD_GROUP = 2
ROW_TILE = 512
FF_TILE = 1024
SAMPLE_KEY_TILE = 512
VMEM_LIMIT_BYTES = 48 * 1024 * 1024


def _params(*sem, flags=None):
    return pltpu.CompilerParams(dimension_semantics=sem, vmem_limit_bytes=VMEM_LIMIT_BYTES, flags=flags)


def _rms(x, g):
    return x * lax.rsqrt(jnp.mean(x * x, axis=-1, keepdims=True) + EPS) * g


def _log_sigmoid(x):
    return jnp.minimum(x, 0.0) - jnp.log1p(jnp.exp(-jnp.abs(x)))


def _dot_nt(a, b):
    return lax.dot_general(a, b, (((1,), (1,)), ((), ())), preferred_element_type=F32)


def _dot_tn(a, b):
    return lax.dot_general(a, b, (((0,), (0,)), ((), ())), preferred_element_type=F32)


def _norm_proj_kernel(x_ref, g_ref, *refs, specs):
    n_w = len(specs)
    w_refs, o_refs = refs[:n_w], refs[n_w:]
    xn = _rms(x_ref[...], g_ref[...]).astype(BF16)
    oi = 0
    for w_ref, (scale, dtypes) in zip(w_refs, specs):
        acc = jnp.dot(xn, w_ref[...], preferred_element_type=F32)
        if scale != 1.0:
            acc = acc * scale
        for dt in dtypes:
            o_refs[oi][...] = acc.astype(dt)
            oi += 1


def _norm_proj(x, g, ws, specs, tm):
    m, d = x.shape
    assert m % tm == 0
    out_shapes, out_specs = [], []
    for w, (_, dtypes) in zip(ws, specs):
        for dt in dtypes:
            out_shapes.append(jax.ShapeDtypeStruct((m, w.shape[1]), dt))
            out_specs.append(pl.BlockSpec((tm, w.shape[1]), lambda i: (i, 0)))
    in_specs = [pl.BlockSpec((tm, d), lambda i: (i, 0)), pl.BlockSpec((1, d), lambda i: (0, 0))]
    in_specs += [pl.BlockSpec(w.shape, lambda i: (0, 0)) for w in ws]
    return pl.pallas_call(
        functools.partial(_norm_proj_kernel, specs=tuple(specs)),
        grid=(m // tm,),
        in_specs=in_specs,
        out_specs=out_specs,
        out_shape=out_shapes,
        compiler_params=_params("parallel"),
        name="norm_proj",
    )(x, g, *ws)


def _mix_mlp_kernel(h_ref, x_ref, wo_ref, gmix_ref, gpre_ref, wup_ref, wdn_ref, gpost_ref, o_ref, *, tf):
    x1 = x_ref[...] + _rms(jnp.dot(h_ref[...], wo_ref[...], preferred_element_type=F32), gmix_ref[...])
    xn = _rms(x1, gpre_ref[...]).astype(BF16)
    acc = None
    for c in range(wup_ref.shape[1] // tf):
        up = jnp.maximum(jnp.dot(xn, wup_ref[:, c * tf:(c + 1) * tf], preferred_element_type=F32), 0.0)
        part = jnp.dot((up * up).astype(BF16), wdn_ref[c * tf:(c + 1) * tf, :], preferred_element_type=F32)
        acc = part if acc is None else acc + part
    o_ref[...] = x1 + _rms(acc, gpost_ref[...])


def _mix_mlp(h, x, w_o, g_mix, g_pre, w_up, w_down, g_post, tm, tf):
    m, d = x.shape
    k = h.shape[1]
    ff = w_up.shape[1]
    assert m % tm == 0 and ff % tf == 0
    resident = lambda shape: pl.BlockSpec(shape, lambda i: (0, 0), pipeline_mode=pl.Buffered(1))
    return pl.pallas_call(
        functools.partial(_mix_mlp_kernel, tf=tf),
        grid=(m // tm,),
        in_specs=[pl.BlockSpec((tm, k), lambda i: (i, 0)),
                  pl.BlockSpec((tm, d), lambda i: (i, 0)),
                  resident((k, d)),
                  resident((1, d)),
                  resident((1, d)),
                  resident((d, ff)),
                  resident((ff, d)),
                  resident((1, d))],
        out_specs=pl.BlockSpec((tm, d), lambda i: (i, 0)),
        out_shape=jax.ShapeDtypeStruct((m, d), F32),
        compiler_params=_params("parallel"),
        name="mix_mlp",
    )(h, x, w_o, g_mix, g_pre, w_up, w_down, g_post)


def _mlstm_kernel(q_ref, k_ref, v_ref, o_ref, gate_ref, bcol_ref, gh_ref, c0_ref, n0_ref, m0_ref,
                  hg_ref, c_ref, n_ref, m_ref, *, nb, blk, heads, dk, dv):
    t = pl.program_id(1)

    @pl.when(t == 0)
    def _():
        c_ref[...] = c0_ref[...]
        n_ref[...] = n0_ref[...]
        m_ref[...] = m0_ref[...]

    row = lax.broadcasted_iota(jnp.int32, (blk, blk), 0)
    col = lax.broadcasted_iota(jnp.int32, (blk, blk), 1)
    causal = row >= col
    triu = (row <= col).astype(F32)

    def square(a):
        if a.shape[1] < LANES:
            a = jnp.concatenate([a, jnp.zeros((a.shape[0], LANES - a.shape[1]), F32)], axis=1)
        if a.shape[0] < LANES:
            a = jnp.concatenate([a, jnp.zeros((LANES - a.shape[0], LANES), F32)], axis=0)
        return a

    pairs = [(b, h) for b in range(nb) for h in range(heads)]
    hr = range(len(pairs))
    qs = [q_ref[b, :, h * dk:(h + 1) * dk] for b, h in pairs]
    ks = [k_ref[b, :, h * dk:(h + 1) * dk] for b, h in pairs]
    vs = [v_ref[b, :, h * dv:(h + 1) * dv] for b, h in pairs]
    c0s = [c_ref[b, h] for b, h in pairs]
    n0s = [n_ref[b, h] for b, h in pairs]
    m0s = [m_ref[b, h:h + 1, :] for b, h in pairs]
    ones = jnp.ones((blk, LANES), BF16)
    v1s = [jnp.concatenate([vs[h], ones], axis=1) for h in hr]
    cns = [jnp.concatenate([c0s[h], n0s[h]], axis=1).astype(BF16) for h in hr]

    qk = [_dot_nt(qs[h], ks[h]) for h in hr]
    qcn = [jnp.dot(qs[h], cns[h], preferred_element_type=F32) for h in hr]

    stats_cols, r_rows = [], []
    for b in range(nb):
        gates_row = square(gate_ref[b]).T[:, :blk]
        z_row = gates_row[0:2 * heads, :] + bcol_ref[...]
        li_row = z_row[0:heads, :]
        lf_row = _log_sigmoid(z_row[heads:2 * heads, :])
        fcum_row = jnp.dot(lf_row, triu, preferred_element_type=F32, precision=lax.Precision.HIGHEST)
        r_rows.append(li_row - fcum_row)
        stats_cols.append(square(jnp.concatenate([li_row, fcum_row], axis=0)).T[:blk, :])
    r_row = [r_rows[b][h:h + 1, :] for b, h in pairs]
    fcs = [jnp.broadcast_to(stats_cols[b][:, heads + h:heads + h + 1], (blk, LANES)) for b, h in pairs]
    lis = [jnp.broadcast_to(stats_cols[b][:, h:h + 1], (blk, LANES)) for b, h in pairs]

    dmat = [jnp.where(causal, fcs[h][:, :blk] + r_row[h], NEG_INF) for h in hr]
    a = [fcs[h] + m0s[h] for h in hr]
    m_new = [jnp.maximum(a[h], jnp.max(dmat[h], axis=1, keepdims=True)) for h in hr]
    s = [qk[h] * jnp.exp(dmat[h] - m_new[h][:, :blk]) for h in hr]
    inter = [jnp.exp(a[h] - m_new[h]) for h in hr]
    sv1 = [jnp.dot(s[h].astype(BF16), v1s[h], preferred_element_type=F32) for h in hr]
    qn = [sv1[h][:, dv:] + inter[h] * qcn[h][:, dv:] for h in hr]
    hh = [(sv1[h][:, :dv] + inter[h] * qcn[h][:, :dv]) / jnp.maximum(jnp.abs(qn[h]), jnp.exp(-m_new[h]))
          for h in hr]

    for i, (b, h) in enumerate(pairs):
        hn = _rms(hh[i], gh_ref[:, h * dv:(h + 1) * dv])
        og = o_ref[b, :, h * dv:(h + 1) * dv].astype(F32)
        hg_ref[b, :, h * dv:(h + 1) * dv] = (hn * jax.nn.sigmoid(og)).astype(BF16)

    m_last = [m_new[h][blk - 1:blk, :] for h in hr]
    w_last = [jnp.exp(fcs[h][blk - 1:blk, :] - fcs[h] + lis[h] - m_last[h]) for h in hr]
    decay = [jnp.exp(a[h][blk - 1:blk, :] - m_last[h]) for h in hr]
    kw = [ks[h].astype(F32) * w_last[h][:, :dk] for h in hr]
    kv1 = [_dot_tn(kw[h].astype(BF16), v1s[h]) for h in hr]
    for i, (b, h) in enumerate(pairs):
        c_ref[b, h] = decay[i] * c0s[i] + kv1[i][:, :dv]
        n_ref[b, h] = decay[i] * n0s[i] + kv1[i][:, dv:]
        m_ref[b, h:h + 1, :] = m_last[i]


def _mlstm(q, k, v, o, gates, b_col, g_h, c0, n0, m0, blk, nb):
    bsz, seqlen, _ = q.shape
    heads, dk, dv = c0.shape[1], c0.shape[2], c0.shape[3]
    assert seqlen % blk == 0 and bsz % nb == 0 and dv == LANES
    row_map = lambda b, t: (b, t, 0)
    const2 = lambda b, t: (0, 0)
    st4 = lambda b, t: (b, 0, 0, 0)
    st3 = lambda b, t: (b, 0, 0)
    return pl.pallas_call(
        functools.partial(_mlstm_kernel, nb=nb, blk=blk, heads=heads, dk=dk, dv=dv),
        grid=(bsz // nb, seqlen // blk),
        in_specs=[pl.BlockSpec((nb, blk, heads * dk), row_map),
                  pl.BlockSpec((nb, blk, heads * dk), row_map),
                  pl.BlockSpec((nb, blk, heads * dv), row_map),
                  pl.BlockSpec((nb, blk, heads * dv), row_map),
                  pl.BlockSpec((nb, blk, LANES), row_map),
                  pl.BlockSpec((2 * heads, 1), const2),
                  pl.BlockSpec((1, heads * dv), const2),
                  pl.BlockSpec((nb, heads, dk, dv), st4),
                  pl.BlockSpec((nb, heads, dk, LANES), st4),
                  pl.BlockSpec((nb, heads, LANES), st3)],
        out_specs=[pl.BlockSpec((nb, blk, heads * dv), row_map),
                   pl.BlockSpec((nb, heads, dk, dv), st4),
                   pl.BlockSpec((nb, heads, dk, LANES), st4),
                   pl.BlockSpec((nb, heads, LANES), st3)],
        out_shape=[jax.ShapeDtypeStruct((bsz, seqlen, heads * dv), BF16),
                   jax.ShapeDtypeStruct((bsz, heads, dk, dv), F32),
                   jax.ShapeDtypeStruct((bsz, heads, dk, LANES), F32),
                   jax.ShapeDtypeStruct((bsz, heads, LANES), F32)],
        compiler_params=_params("parallel", "arbitrary"),
        name="mlstm",
    )(q, k, v, o, gates, b_col, g_h, c0, n0, m0)


def _stack_maps(q):
    lane = lax.broadcasted_iota(jnp.int32, q.shape, 1)
    zero = jnp.zeros_like(q)
    return jnp.concatenate([jnp.where(lane < HD_B, q, zero), jnp.where(lane >= HD_B, q, zero)], axis=0)


def _lambda(lam_ref, lam_init):
    lv = lam_ref[...]
    a = jnp.sum(lv[0:1] * lv[1:2], axis=1, keepdims=True)
    b = jnp.sum(lv[2:3] * lv[3:4], axis=1, keepdims=True)
    return jnp.exp(a) - jnp.exp(b) + lam_init


def _finish_head(acc, l, lam, g_sub, lam_init, tq):
    w = acc / l
    o = w[0:tq] - lam * w[tq:2 * tq]
    return _rms(o, g_sub) * (1.0 - lam_init)


def _attn_prompt_kernel(q_ref, kt_ref, v_ref, lam_ref, gsub_ref, o_ref, qs_sc, m_sc, l_sc, acc_sc,
                        *, tile, hg, lam_init):
    i = pl.program_id(2)
    heads = range(hg)
    for g in heads:
        qs_sc[g] = _stack_maps(q_ref[0, :, g * LANES:(g + 1) * LANES])
    m_sc[...] = jnp.full(m_sc.shape, NEG_INF, F32)
    l_sc[...] = jnp.zeros_like(l_sc)
    acc_sc[...] = jnp.zeros_like(acc_sc)
    n_lane_tiles = tile // LANES

    def step(j, masked):
        ss = [jnp.dot(qs_sc[g], kt_ref[0, j, g * LANES:(g + 1) * LANES, :], preferred_element_type=F32)
              for g in heads]
        if masked:
            row = lax.broadcasted_iota(jnp.int32, ss[0].shape, 0) & (tile - 1)
            col = lax.broadcasted_iota(jnp.int32, ss[0].shape, 1)
            ss = [jnp.where((col // CHUNK) <= (row // CHUNK), s, NEG_INF) for s in ss]
        m_prev = [m_sc[g] for g in heads]
        m_new = [jnp.maximum(m_prev[g], jnp.max(ss[g], axis=1, keepdims=True)) for g in heads]
        alpha = [jnp.exp2(m_prev[g] - m_new[g]) for g in heads]
        ps = [jnp.exp2((ss[g] - jnp.concatenate([m_new[g]] * n_lane_tiles, axis=1)).astype(BF16)) for g in heads]
        start = pl.multiple_of(j * tile, tile)
        pv = [jnp.dot(ps[g], v_ref[0, g, pl.ds(start, tile), :], preferred_element_type=F32) for g in heads]
        for g in heads:
            p_sum = ps[g][:, 0:LANES]
            for c in range(1, n_lane_tiles):
                p_sum = p_sum + ps[g][:, c * LANES:(c + 1) * LANES]
            l_sc[g] = alpha[g] * l_sc[g] + p_sum.astype(F32)
            acc_sc[g] = alpha[g] * acc_sc[g] + pv[g]
            m_sc[g] = m_new[g]

    def body(jj, carry):
        for u in range(KEY_LOOP_UNROLL):
            step(KEY_LOOP_UNROLL * jj + u, False)
        return carry

    n_full = i // KEY_LOOP_UNROLL
    lax.fori_loop(0, n_full, body, 0)
    for rem in range(1, KEY_LOOP_UNROLL):
        @pl.when(i - KEY_LOOP_UNROLL * n_full >= rem)
        def _():
            step(KEY_LOOP_UNROLL * n_full + rem - 1, False)
    step(i, True)
    lam = _lambda(lam_ref, lam_init)
    for g in heads:
        l = jnp.sum(l_sc[g], axis=1, keepdims=True)
        o = _finish_head(acc_sc[g], l, lam, gsub_ref[:, g * LANES:(g + 1) * LANES], lam_init, tile)
        o_ref[0, :, g * LANES:(g + 1) * LANES] = o.astype(BF16)


def _attn_prompt(q, kt, v, lam_vec, g_sub, lam_init, hg):
    bsz, seqlen, width = q.shape
    tile = kt.shape[3]
    heads = width // LANES
    gw = hg * LANES
    assert seqlen % tile == 0 and tile % CHUNK == 0 and (tile & (tile - 1)) == 0 and tile % LANES == 0
    assert heads % hg == 0
    return pl.pallas_call(
        functools.partial(_attn_prompt_kernel, tile=tile, hg=hg, lam_init=lam_init),
        grid=(bsz, heads // hg, seqlen // tile),
        in_specs=[pl.BlockSpec((1, tile, gw), lambda b, h, i: (b, i, h)),
                  pl.BlockSpec((1, seqlen // tile, gw, tile), lambda b, h, i: (b, 0, h, 0)),
                  pl.BlockSpec((1, hg, seqlen, LANES), lambda b, h, i: (b, h, 0, 0)),
                  pl.BlockSpec((4, HD_B), lambda b, h, i: (0, 0)),
                  pl.BlockSpec((1, gw), lambda b, h, i: (0, h))],
        out_specs=pl.BlockSpec((1, tile, gw), lambda b, h, i: (b, i, h)),
        out_shape=jax.ShapeDtypeStruct((bsz, seqlen, width), BF16),
        scratch_shapes=[pltpu.VMEM((hg, 2 * tile, LANES), BF16),
                        pltpu.VMEM((hg, 2 * tile, LANES), F32),
                        pltpu.VMEM((hg, 2 * tile, LANES), F32),
                        pltpu.VMEM((hg, 2 * tile, LANES), F32)],
        compiler_params=_params("parallel", "parallel", "arbitrary"),
        name="attn_prompt",
    )(q, kt, v, lam_vec, g_sub)


def _kv_proj_t_kernel(x_ref, g_ref, gq_ref, wkt_ref, wv_ref, wq_ref, kt_ref, ktb_ref, v_ref, vb_ref, q_ref,
                      *, heads, q_scale):
    x = x_ref[...]
    xr = x * lax.rsqrt(jnp.mean(x * x, axis=-1, keepdims=True) + EPS)
    q = jnp.dot((xr * gq_ref[...]).astype(BF16), wq_ref[...], preferred_element_type=F32)
    q_ref[...] = (q * q_scale).astype(BF16)
    xn = (xr * g_ref[...]).astype(BF16)
    kt = _dot_nt(wkt_ref[...], xn)
    kt_ref[0] = kt
    ktb_ref[0, 0] = kt.astype(BF16)
    v = jnp.dot(xn, wv_ref[...], preferred_element_type=F32)
    v_ref[...] = v
    for h in range(heads):
        vb_ref[0, h] = v[:, h * LANES:(h + 1) * LANES].astype(BF16)


def _kv_proj_t(x, g_kv, g_q, wkt, wv, wq, q_scale, bsz, tm):
    rows, d = x.shape
    seqlen = rows // bsz
    width = wv.shape[1]
    heads = width // LANES
    assert seqlen % tm == 0
    nt = seqlen // tm
    row_blk = lambda b, i: (b * nt + i, 0)
    const = lambda b, i: (0, 0)
    return pl.pallas_call(
        functools.partial(_kv_proj_t_kernel, heads=heads, q_scale=q_scale),
        grid=(bsz, nt),
        in_specs=[pl.BlockSpec((tm, d), row_blk),
                  pl.BlockSpec((1, d), const),
                  pl.BlockSpec((1, d), const),
                  pl.BlockSpec((width, d), const),
                  pl.BlockSpec((d, width), const),
                  pl.BlockSpec((d, width), const)],
        out_specs=[pl.BlockSpec((1, width, tm), lambda b, i: (b, 0, i)),
                   pl.BlockSpec((1, 1, width, tm), lambda b, i: (b, i, 0, 0)),
                   pl.BlockSpec((tm, width), row_blk),
                   pl.BlockSpec((1, heads, tm, LANES), lambda b, i: (b, 0, i, 0)),
                   pl.BlockSpec((tm, width), row_blk)],
        out_shape=[jax.ShapeDtypeStruct((bsz, width, seqlen), F32),
                   jax.ShapeDtypeStruct((bsz, nt, width, tm), BF16),
                   jax.ShapeDtypeStruct((rows, width), F32),
                   jax.ShapeDtypeStruct((bsz, heads, seqlen, LANES), BF16),
                   jax.ShapeDtypeStruct((rows, width), BF16)],
        compiler_params=_params("parallel", "parallel"),
        name="kv_proj_t",
    )(x, g_kv, g_q, wkt, wv, wq)


def _attn_sample_kernel(q_ref, ckt_ref, cv_ref, kn_ref, vn_ref, lam_ref, gsub_ref, o_ref, qs_sc, m_sc, l_sc, acc_sc,
                        *, lq, heads, past, lam_init):
    j = pl.program_id(1)

    @pl.when(j == 0)
    def _():
        for h in range(heads):
            qs_sc[h] = _stack_maps(q_ref[0, :, h * LANES:(h + 1) * LANES])
        m_sc[...] = jnp.full(m_sc.shape, NEG_INF, F32)
        l_sc[...] = jnp.zeros_like(l_sc)
        acc_sc[...] = jnp.zeros_like(acc_sc)

    def update(h, s, vt):
        m_prev = m_sc[h]
        m_new = jnp.maximum(m_prev, jnp.max(s, axis=1, keepdims=True))
        alpha = jnp.exp2(m_prev - m_new)
        p = jnp.exp2(s - m_new)
        l_sc[h] = alpha * l_sc[h] + jnp.sum(p, axis=1, keepdims=True)
        acc_sc[h] = alpha * acc_sc[h] + jnp.dot(p.astype(BF16), vt, preferred_element_type=F32)
        m_sc[h] = m_new

    for h in range(heads):
        kt = ckt_ref[0, h * LANES:(h + 1) * LANES, :].astype(BF16)
        update(h, jnp.dot(qs_sc[h], kt, preferred_element_type=F32), cv_ref[0, :, h * LANES:(h + 1) * LANES])

    @pl.when(j == pl.num_programs(1) - 1)
    def _():
        row = lax.broadcasted_iota(jnp.int32, (2 * lq, lq), 0)
        row = jnp.where(row >= lq, row - lq, row)
        col = lax.broadcasted_iota(jnp.int32, (2 * lq, lq), 1)
        mask = ((past + col) // CHUNK) <= ((past + row) // CHUNK)
        lam = _lambda(lam_ref, lam_init)
        for h in range(heads):
            hs = slice(h * LANES, (h + 1) * LANES)
            update(h, jnp.where(mask, _dot_nt(qs_sc[h], kn_ref[0, :, hs]), NEG_INF), vn_ref[0, :, hs])
            o = _finish_head(acc_sc[h], l_sc[h], lam, gsub_ref[:, hs], lam_init, lq)
            o_ref[0, :, hs] = o.astype(BF16)


def _attn_sample(q, cache_kt, cache_v, k_new, v_new, lam_vec, g_sub, lam_init, tkc):
    bsz, lq, width = q.shape
    past = cache_kt.shape[2]
    heads = width // LANES
    assert past % tkc == 0 and past % CHUNK == 0
    new_blk = pl.BlockSpec((1, lq, width), lambda b, j: (b, 0, 0))
    return pl.pallas_call(
        functools.partial(_attn_sample_kernel, lq=lq, heads=heads, past=past, lam_init=lam_init),
        grid=(bsz, past // tkc),
        in_specs=[new_blk,
                  pl.BlockSpec((1, width, tkc), lambda b, j: (b, 0, j)),
                  pl.BlockSpec((1, tkc, width), lambda b, j: (b, j, 0)),
                  new_blk,
                  new_blk,
                  pl.BlockSpec((4, HD_B), lambda b, j: (0, 0)),
                  pl.BlockSpec((1, width), lambda b, j: (0, 0))],
        out_specs=new_blk,
        out_shape=jax.ShapeDtypeStruct((bsz, lq, width), BF16),
        scratch_shapes=[pltpu.VMEM((heads, 2 * lq, LANES), BF16),
                        pltpu.VMEM((heads, 2 * lq, 1), F32),
                        pltpu.VMEM((heads, 2 * lq, 1), F32),
                        pltpu.VMEM((heads, 2 * lq, LANES), F32)],
        compiler_params=_params("parallel", "arbitrary"),
        name="attn_sample",
    )(q, cache_kt, cache_v, k_new, v_new, lam_vec, g_sub)


def _lambda_init(layer_idx):
    return 0.8 - 0.6 * math.exp(-0.3 * layer_idx)


def _trunk(x3, state, cache, p, mlstm_blk):
    bsz, seqlen, d = x3.shape
    rows = bsz * seqlen
    tm = min(ROW_TILE, rows)
    x = x3.reshape(rows, d)
    row = lambda g: g.reshape(1, -1)

    qk_w = H_A * p["dk"]
    v_w = H_A * p["dv"]
    w_in = p["w_in_a"]
    w_parts = [w_in[:, 0:qk_w], w_in[:, qk_w:2 * qk_w], w_in[:, 2 * qk_w:2 * qk_w + v_w],
               w_in[:, 2 * qk_w + v_w:2 * qk_w + 2 * v_w]]
    w_gate = jnp.pad(w_in[:, 2 * qk_w + 2 * v_w:], ((0, 0), (0, LANES - 2 * H_A)))
    ws = [w.astype(BF16) for w in w_parts] + [w_gate.astype(BF16)]
    specs = [(1.0, (BF16,)), (p["dk"] ** -0.5, (BF16,)), (1.0, (BF16,)), (1.0, (BF16,)), (1.0, (F32,))]
    q, k, v, o, gates = _norm_proj(x, row(p["g_mix_pre"][0]), ws, specs, tm)

    bias = jnp.concatenate([p["b_i_a"], p["b_f_a"]]).astype(F32)
    b_col = bias.reshape(2 * H_A, 1)
    c0, n0, m0 = state
    m0b = jnp.broadcast_to(m0[:, :, None], (bsz, H_A, LANES))
    n0b = jnp.broadcast_to(n0[:, :, :, None], n0.shape + (LANES,))
    r3 = lambda a: a.reshape(bsz, seqlen, a.shape[-1])
    hg, c1, n1b, m1 = _mlstm(r3(q), r3(k), r3(v), r3(o), r3(gates), b_col, row(p["g_h_a"]), c0, n0b, m0b,
                             mlstm_blk, MLSTM_STREAMS)
    n1 = n1b[:, :, :, 0]
    x = _mix_mlp(hg.reshape(rows, v_w), x, p["w_out_a"].astype(BF16), row(p["g_mix_post"][0]),
                 row(p["g_ffn_pre"][0]), p["w_up"][0].astype(BF16), p["w_down"][0].astype(BF16),
                 row(p["g_ffn_post"][0]), tm, FF_TILE)

    kb_w = H_B * 2 * HD_B
    w_k = p["w_kv"][:, :kb_w]
    w_v = p["w_kv"][:, kb_w:].astype(BF16)
    q_scale = HD_B ** -0.5 * math.log2(math.e)
    w_q = p["w_q_b"].astype(BF16)
    g_q = row(p["g_mix_pre"][1])

    lam_init = _lambda_init(1)
    lam_vec = jnp.stack([p["lambda_q1"], p["lambda_k1"], p["lambda_q2"], p["lambda_k2"]]).astype(F32)
    g_sub = row(p["g_sub_b"])
    shape3 = (bsz, seqlen, kb_w)
    if cache is None:
        assert ATTN_TILE == tm
        kt_new, kt_bf, v_new, v_bf, qb = _kv_proj_t(x, row(p["g_kv"]), g_q, w_k.T.astype(BF16), w_v, w_q, q_scale,
                                                    bsz, ATTN_TILE)
        k_out = kt_new.reshape(bsz, H_B, 2, HD_B, seqlen).transpose(0, 4, 1, 2, 3)
        att = _attn_prompt(qb.reshape(shape3), kt_bf, v_bf, lam_vec, g_sub, lam_init, ATTN_HEAD_GROUP)
    else:
        (qb,) = _norm_proj(x, g_q, [w_q], [(q_scale, (BF16,))], tm)
        k_new, k_bf, v_new, v_bf = _norm_proj(x, row(p["g_kv"]), [w_k.astype(BF16), w_v],
                                              [(1.0, (F32, BF16)), (1.0, (F32, BF16))], tm)
        k_out = k_new.reshape(bsz, seqlen, H_B, 2, HD_B)
        ck, cv = cache
        past = ck.shape[1]
        ckt = ck.transpose(0, 2, 3, 4, 1).reshape(bsz, kb_w, past)
        cvb = cv.astype(BF16).reshape(bsz, past, kb_w)
        att = _attn_sample(qb.reshape(shape3), ckt, cvb, k_bf.reshape(shape3), v_bf.reshape(shape3),
                           lam_vec, g_sub, lam_init, min(SAMPLE_KEY_TILE, past))
    x = _mix_mlp(att.reshape(rows, kb_w), x, p["w_o_b"].astype(BF16), row(p["g_mix_post"][1]),
                 row(p["g_ffn_pre"][1]), p["w_up"][1].astype(BF16), p["w_down"][1].astype(BF16),
                 row(p["g_ffn_post"][1]), tm, FF_TILE)

    y = x.reshape(bsz, seqlen, d)
    v_out = v_new.reshape(bsz, seqlen, H_B, 2 * HD_B)
    return y, (c1[None], n1[None], m1[None, :, :, 0]), k_out, v_out


def kernel(x_prompt, x_sample, cache_k, cache_v, state_c, state_n, state_m, w_in_a, b_i_a, b_f_a, g_h_a, w_out_a,
           g_kv, w_kv, w_q_b, lambda_q1, lambda_k1, lambda_q2, lambda_k2, g_sub_b, w_o_b, g_mix_pre, g_mix_post,
           g_ffn_pre, g_ffn_post, w_up, w_down):
    assert w_in_a.shape[0] == 1 and w_q_b.shape[0] == 1 and g_mix_pre.shape[0] == 2
    dk, dv = state_c.shape[3], state_c.shape[4]
    p = {"dk": dk, "dv": dv,
         "w_in_a": w_in_a[0], "b_i_a": b_i_a[0], "b_f_a": b_f_a[0], "g_h_a": g_h_a[0], "w_out_a": w_out_a[0],
         "g_kv": g_kv, "w_kv": w_kv, "w_q_b": w_q_b[0],
         "lambda_q1": lambda_q1[0], "lambda_k1": lambda_k1[0], "lambda_q2": lambda_q2[0], "lambda_k2": lambda_k2[0],
         "g_sub_b": g_sub_b[0], "w_o_b": w_o_b[0], "g_mix_pre": g_mix_pre, "g_mix_post": g_mix_post,
         "g_ffn_pre": g_ffn_pre, "g_ffn_post": g_ffn_post, "w_up": w_up, "w_down": w_down}
    bsz = x_prompt.shape[0]
    zero_state = (jnp.zeros((bsz, H_A, dk, dv), F32), jnp.zeros((bsz, H_A, dk), F32), jnp.zeros((bsz, H_A), F32))
    y_p, (c_p, n_p, m_p), k_p, v_p = _trunk(x_prompt, zero_state, None, p, MLSTM_BLOCK)
    y_s, (c_s, n_s, m_s), k_s, v_s = _trunk(
        x_sample, (state_c[0].astype(F32), state_n[0].astype(F32), state_m[0].astype(F32)),
        (cache_k, cache_v), p, x_sample.shape[1])
    return (y_p, y_s, c_p, n_p, m_p, k_p, v_p, c_s, n_s, m_s, k_s, v_s)
```

```python
import functools
import math

import jax
import jax.numpy as jnp
from jax import lax
from jax.experimental import pallas as pl
from jax.experimental.pallas import tpu as pltpu

F32 = jnp.float32
BF16 = jnp.bfloat16

EPS = 1e-6
NEG_INF = -1e30
CHUNK = 64
LANES = 128
H_A = 8
H_B = 8
HD_B = 64
MLSTM_BLOCK = 128
MLSTM_HEAD_GROUP = 8
MLSTM_STREAMS = 1
ATTN_TILE = 512
KEY_LOOP_UNROLL = 2
ATTN_HEAD_GROUP = 2
ROW_TILE = 512
FF_TILE = 1024
SAMPLE_KEY_TILE = 512
VMEM_LIMIT_BYTES = 48 * 1024 * 1024


def _params(*sem, flags=None):
    return pltpu.CompilerParams(dimension_semantics=sem, vmem_limit_bytes=VMEM_LIMIT_BYTES, flags=flags)


def _rms(x, g):
    return x * lax.rsqrt(jnp.mean(x * x, axis=-1, keepdims=True) + EPS) * g


def _log_sigmoid(x):
    return jnp.minimum(x, 0.0) - jnp.log1p(jnp.exp(-jnp.abs(x)))


def _dot_nt(a, b):
    return lax.dot_general(a, b, (((1,), (1,)), ((), ())), preferred_element_type=F32)


def _dot_tn(a, b):
    return lax.dot_general(a, b, (((0,), (0,)), ((), ())), preferred_element_type=F32)


def _norm_proj_kernel(x_ref, g_ref, *refs, specs):
    n_w = len(specs)
    w_refs, o_refs = refs[:n_w], refs[n_w:]
    xn = _rms(x_ref[...], g_ref[...]).astype(BF16)
    oi = 0
    for w_ref, (scale, dtypes) in zip(w_refs, specs):
        acc = jnp.dot(xn, w_ref[...], preferred_element_type=F32)
        if scale != 1.0:
            acc = acc * scale
        for dt in dtypes:
            o_refs[oi][...] = acc.astype(dt)
            oi += 1


def _norm_proj(x, g, ws, specs, tm):
    m, d = x.shape
    assert m % tm == 0
    out_shapes, out_specs = [], []
    for w, (_, dtypes) in zip(ws, specs):
        for dt in dtypes:
            out_shapes.append(jax.ShapeDtypeStruct((m, w.shape[1]), dt))
            out_specs.append(pl.BlockSpec((tm, w.shape[1]), lambda i: (i, 0)))
    in_specs = [pl.BlockSpec((tm, d), lambda i: (i, 0)), pl.BlockSpec((1, d), lambda i: (0, 0))]
    in_specs += [pl.BlockSpec(w.shape, lambda i: (0, 0)) for w in ws]
    return pl.pallas_call(
        functools.partial(_norm_proj_kernel, specs=tuple(specs)),
        grid=(m // tm,),
        in_specs=in_specs,
        out_specs=out_specs,
        out_shape=out_shapes,
        compiler_params=_params("parallel"),
        name="norm_proj",
    )(x, g, *ws)


def _mix_mlp_kernel(h_ref, x_ref, wo_ref, gmix_ref, gpre_ref, wup_ref, wdn_ref, gpost_ref, o_ref, *, tf):
    x1 = x_ref[...] + _rms(jnp.dot(h_ref[...], wo_ref[...], preferred_element_type=F32), gmix_ref[...])
    xn = _rms(x1, gpre_ref[...]).astype(BF16)
    acc = None
    for c in range(wup_ref.shape[1] // tf):
        up = jnp.maximum(jnp.dot(xn, wup_ref[:, c * tf:(c + 1) * tf], preferred_element_type=F32), 0.0)
        part = jnp.dot((up * up).astype(BF16), wdn_ref[c * tf:(c + 1) * tf, :], preferred_element_type=F32)
        acc = part if acc is None else acc + part
    o_ref[...] = x1 + _rms(acc, gpost_ref[...])


def _mix_mlp(h, x, w_o, g_mix, g_pre, w_up, w_down, g_post, tm, tf):
    m, d = x.shape
    k = h.shape[1]
    ff = w_up.shape[1]
    assert m % tm == 0 and ff % tf == 0
    resident = lambda shape: pl.BlockSpec(shape, lambda i: (0, 0), pipeline_mode=pl.Buffered(1))
    return pl.pallas_call(
        functools.partial(_mix_mlp_kernel, tf=tf),
        grid=(m // tm,),
        in_specs=[pl.BlockSpec((tm, k), lambda i: (i, 0)),
                  pl.BlockSpec((tm, d), lambda i: (i, 0)),
                  resident((k, d)),
                  resident((1, d)),
                  resident((1, d)),
                  resident((d, ff)),
                  resident((ff, d)),
                  resident((1, d))],
        out_specs=pl.BlockSpec((tm, d), lambda i: (i, 0)),
        out_shape=jax.ShapeDtypeStruct((m, d), F32),
        compiler_params=_params("parallel"),
        name="mix_mlp",
    )(h, x, w_o, g_mix, g_pre, w_up, w_down, g_post)


def _mlstm_kernel(q_ref, k_ref, v_ref, o_ref, gate_ref, bcol_ref, gh_ref, c0_ref, n0_ref, m0_ref,
                  hg_ref, c_ref, n_ref, m_ref, *, nb, blk, heads, dk, dv):
    t = pl.program_id(1)

    @pl.when(t == 0)
    def _():
        c_ref[...] = c0_ref[...]
        n_ref[...] = n0_ref[...]
        m_ref[...] = m0_ref[...]

    row = lax.broadcasted_iota(jnp.int32, (blk, blk), 0)
    col = lax.broadcasted_iota(jnp.int32, (blk, blk), 1)
    causal = row >= col
    triu = (row <= col).astype(F32)

    def square(a):
        if a.shape[1] < LANES:
            a = jnp.concatenate([a, jnp.zeros((a.shape[0], LANES - a.shape[1]), F32)], axis=1)
        if a.shape[0] < LANES:
            a = jnp.concatenate([a, jnp.zeros((LANES - a.shape[0], LANES), F32)], axis=0)
        return a

    all_pairs = [(b, h) for b in range(nb) for h in range(heads)]
    for g0 in range(0, len(all_pairs), MLSTM_HEAD_GROUP):
        _mlstm_group(all_pairs[g0:g0 + MLSTM_HEAD_GROUP], q_ref, k_ref, v_ref, o_ref, gate_ref, bcol_ref, gh_ref,
                     hg_ref, c_ref, n_ref, m_ref, causal, triu, square, blk=blk, heads=heads, dk=dk, dv=dv)


def _mlstm_group(pairs, q_ref, k_ref, v_ref, o_ref, gate_ref, bcol_ref, gh_ref, hg_ref, c_ref, n_ref, m_ref,
                 causal, triu, square, *, blk, heads, dk, dv):
    hr = range(len(pairs))
    streams = sorted({b for b, _ in pairs})
    qs = [q_ref[b, :, h * dk:(h + 1) * dk] for b, h in pairs]
    ks = [k_ref[b, :, h * dk:(h + 1) * dk] for b, h in pairs]
    vs = [v_ref[b, :, h * dv:(h + 1) * dv] for b, h in pairs]
    c0s = [c_ref[b, h] for b, h in pairs]
    n0s = [n_ref[b, h] for b, h in pairs]
    m0s = [m_ref[b, h:h + 1, :] for b, h in pairs]
    ones = jnp.ones((blk, LANES), BF16)
    v1s = [jnp.concatenate([vs[h], ones], axis=1) for h in hr]
    cns = [jnp.concatenate([c0s[h], n0s[h]], axis=1).astype(BF16) for h in hr]

    qk = [_dot_nt(qs[h], ks[h]) for h in hr]
    qcn = [jnp.dot(qs[h], cns[h], preferred_element_type=F32) for h in hr]

    stats_cols, r_rows = {}, {}
    for b in streams:
        gates_row = square(gate_ref[b]).T[:, :blk]
        z_row = gates_row[0:2 * heads, :] + bcol_ref[...]
        li_row = z_row[0:heads, :]
        lf_row = _log_sigmoid(z_row[heads:2 * heads, :])
        fcum_row = jnp.dot(lf_row, triu, preferred_element_type=F32, precision=lax.Precision.HIGHEST)
        r_rows[b] = li_row - fcum_row
        stats_cols[b] = square(jnp.concatenate([li_row, fcum_row], axis=0)).T[:blk, :]
    r_row = [r_rows[b][h:h + 1, :] for b, h in pairs]
    fcs = [jnp.broadcast_to(stats_cols[b][:, heads + h:heads + h + 1], (blk, LANES)) for b, h in pairs]
    lis = [jnp.broadcast_to(stats_cols[b][:, h:h + 1], (blk, LANES)) for b, h in pairs]

    dmat = [jnp.where(causal, fcs[h][:, :blk] + r_row[h], NEG_INF) for h in hr]
    a = [fcs[h] + m0s[h] for h in hr]
    m_new = [jnp.maximum(a[h], jnp.max(dmat[h], axis=1, keepdims=True)) for h in hr]
    s = [qk[h] * jnp.exp(dmat[h] - m_new[h][:, :blk]) for h in hr]
    inter = [jnp.exp(a[h] - m_new[h]) for h in hr]
    sv1 = [jnp.dot(s[h].astype(BF16), v1s[h], preferred_element_type=F32) for h in hr]
    qn = [sv1[h][:, dv:] + inter[h] * qcn[h][:, dv:] for h in hr]
    hh = [(sv1[h][:, :dv] + inter[h] * qcn[h][:, :dv]) / jnp.maximum(jnp.abs(qn[h]), jnp.exp(-m_new[h]))
          for h in hr]

    for i, (b, h) in enumerate(pairs):
        hn = _rms(hh[i], gh_ref[:, h * dv:(h + 1) * dv])
        og = o_ref[b, :, h * dv:(h + 1) * dv].astype(F32)
        hg_ref[b, :, h * dv:(h + 1) * dv] = (hn * jax.nn.sigmoid(og)).astype(BF16)

    m_last = [m_new[h][blk - 1:blk, :] for h in hr]
    w_last = [jnp.exp(fcs[h][blk - 1:blk, :] - fcs[h] + lis[h] - m_last[h]) for h in hr]
    decay = [jnp.exp(a[h][blk - 1:blk, :] - m_last[h]) for h in hr]
    kw = [ks[h].astype(F32) * w_last[h][:, :dk] for h in hr]
    kv1 = [_dot_tn(kw[h].astype(BF16), v1s[h]) for h in hr]
    for i, (b, h) in enumerate(pairs):
        c_ref[b, h] = decay[i] * c0s[i] + kv1[i][:, :dv]
        n_ref[b, h] = decay[i] * n0s[i] + kv1[i][:, dv:]
        m_ref[b, h:h + 1, :] = m_last[i]


def _mlstm(q, k, v, o, gates, b_col, g_h, c0, n0, m0, blk, nb):
    bsz, seqlen, _ = q.shape
    heads, dk, dv = c0.shape[1], c0.shape[2], c0.shape[3]
    assert seqlen % blk == 0 and bsz % nb == 0 and dv == LANES
    row_map = lambda b, t: (b, t, 0)
    const2 = lambda b, t: (0, 0)
    st4 = lambda b, t: (b, 0, 0, 0)
    st3 = lambda b, t: (b, 0, 0)
    return pl.pallas_call(
        functools.partial(_mlstm_kernel, nb=nb, blk=blk, heads=heads, dk=dk, dv=dv),
        grid=(bsz // nb, seqlen // blk),
        in_specs=[pl.BlockSpec((nb, blk, heads * dk), row_map),
                  pl.BlockSpec((nb, blk, heads * dk), row_map),
                  pl.BlockSpec((nb, blk, heads * dv), row_map),
                  pl.BlockSpec((nb, blk, heads * dv), row_map),
                  pl.BlockSpec((nb, blk, LANES), row_map),
                  pl.BlockSpec((2 * heads, 1), const2),
                  pl.BlockSpec((1, heads * dv), const2),
                  pl.BlockSpec((nb, heads, dk, dv), st4),
                  pl.BlockSpec((nb, heads, dk, LANES), st4),
                  pl.BlockSpec((nb, heads, LANES), st3)],
        out_specs=[pl.BlockSpec((nb, blk, heads * dv), row_map),
                   pl.BlockSpec((nb, heads, dk, dv), st4),
                   pl.BlockSpec((nb, heads, dk, LANES), st4),
                   pl.BlockSpec((nb, heads, LANES), st3)],
        out_shape=[jax.ShapeDtypeStruct((bsz, seqlen, heads * dv), BF16),
                   jax.ShapeDtypeStruct((bsz, heads, dk, dv), F32),
                   jax.ShapeDtypeStruct((bsz, heads, dk, LANES), F32),
                   jax.ShapeDtypeStruct((bsz, heads, LANES), F32)],
        compiler_params=_params("parallel", "arbitrary"),
        name="mlstm",
    )(q, k, v, o, gates, b_col, g_h, c0, n0, m0)


def _stack_maps(q):
    lane = lax.broadcasted_iota(jnp.int32, q.shape, 1)
    zero = jnp.zeros_like(q)
    return jnp.concatenate([jnp.where(lane < HD_B, q, zero), jnp.where(lane >= HD_B, q, zero)], axis=0)


def _lambda(lam_ref, lam_init):
    lv = lam_ref[...]
    a = jnp.sum(lv[0:1] * lv[1:2], axis=1, keepdims=True)
    b = jnp.sum(lv[2:3] * lv[3:4], axis=1, keepdims=True)
    return jnp.exp(a) - jnp.exp(b) + lam_init


def _finish_head(acc, l, lam, g_sub, lam_init, tq):
    w = acc / l
    o = w[0:tq] - lam * w[tq:2 * tq]
    return _rms(o, g_sub) * (1.0 - lam_init)


def _attn_prompt_kernel(q_ref, kt_ref, v_ref, lam_ref, gsub_ref, o_ref, qs_sc, m_sc, l_sc, acc_sc,
                        *, tile, hg, lam_init):
    i = pl.program_id(2)
    heads = range(hg)
    for g in heads:
        qs_sc[g] = _stack_maps(q_ref[0, :, g * LANES:(g + 1) * LANES])
    m_sc[...] = jnp.full(m_sc.shape, NEG_INF, F32)
    l_sc[...] = jnp.zeros_like(l_sc)
    acc_sc[...] = jnp.zeros_like(acc_sc)
    n_lane_tiles = tile // LANES

    def step(j, masked):
        ss = [jnp.dot(qs_sc[g], kt_ref[0, j, g * LANES:(g + 1) * LANES, :], preferred_element_type=F32)
              for g in heads]
        if masked:
            row = lax.broadcasted_iota(jnp.int32, ss[0].shape, 0) & (tile - 1)
            col = lax.broadcasted_iota(jnp.int32, ss[0].shape, 1)
            ss = [jnp.where((col // CHUNK) <= (row // CHUNK), s, NEG_INF) for s in ss]
        m_prev = [m_sc[g] for g in heads]
        m_new = [jnp.maximum(m_prev[g], jnp.max(ss[g], axis=1, keepdims=True)) for g in heads]
        alpha = [jnp.exp2(m_prev[g] - m_new[g]) for g in heads]
        ps = [jnp.exp2((ss[g] - jnp.concatenate([m_new[g]] * n_lane_tiles, axis=1)).astype(BF16)) for g in heads]
        start = pl.multiple_of(j * tile, tile)
        pv = [jnp.dot(ps[g], v_ref[0, g, pl.ds(start, tile), :], preferred_element_type=F32) for g in heads]
        for g in heads:
            p_sum = ps[g][:, 0:LANES]
            for c in range(1, n_lane_tiles):
                p_sum = p_sum + ps[g][:, c * LANES:(c + 1) * LANES]
            l_sc[g] = alpha[g] * l_sc[g] + p_sum.astype(F32)
            acc_sc[g] = alpha[g] * acc_sc[g] + pv[g]
            m_sc[g] = m_new[g]

    def body(jj, carry):
        for u in range(KEY_LOOP_UNROLL):
            step(KEY_LOOP_UNROLL * jj + u, False)
        return carry

    n_full = i // KEY_LOOP_UNROLL
    lax.fori_loop(0, n_full, body, 0)
    for rem in range(1, KEY_LOOP_UNROLL):
        @pl.when(i - KEY_LOOP_UNROLL * n_full >= rem)
        def _():
            step(KEY_LOOP_UNROLL * n_full + rem - 1, False)
    step(i, True)
    lam = _lambda(lam_ref, lam_init)
    for g in heads:
        l = jnp.sum(l_sc[g], axis=1, keepdims=True)
        o = _finish_head(acc_sc[g], l, lam, gsub_ref[:, g * LANES:(g + 1) * LANES], lam_init, tile)
        o_ref[0, :, g * LANES:(g + 1) * LANES] = o.astype(BF16)


def _attn_prompt(q, kt, v, lam_vec, g_sub, lam_init, hg):
    bsz, seqlen, width = q.shape
    tile = kt.shape[3]
    heads = width // LANES
    gw = hg * LANES
    assert seqlen % tile == 0 and tile % CHUNK == 0 and (tile & (tile - 1)) == 0 and tile % LANES == 0
    assert heads % hg == 0
    return pl.pallas_call(
        functools.partial(_attn_prompt_kernel, tile=tile, hg=hg, lam_init=lam_init),
        grid=(bsz, heads // hg, seqlen // tile),
        in_specs=[pl.BlockSpec((1, tile, gw), lambda b, h, i: (b, i, h)),
                  pl.BlockSpec((1, seqlen // tile, gw, tile), lambda b, h, i: (b, 0, h, 0)),
                  pl.BlockSpec((1, hg, seqlen, LANES), lambda b, h, i: (b, h, 0, 0)),
                  pl.BlockSpec((4, HD_B), lambda b, h, i: (0, 0)),
                  pl.BlockSpec((1, gw), lambda b, h, i: (0, h))],
        out_specs=pl.BlockSpec((1, tile, gw), lambda b, h, i: (b, i, h)),
        out_shape=jax.ShapeDtypeStruct((bsz, seqlen, width), BF16),
        scratch_shapes=[pltpu.VMEM((hg, 2 * tile, LANES), BF16),
                        pltpu.VMEM((hg, 2 * tile, LANES), F32),
                        pltpu.VMEM((hg, 2 * tile, LANES), F32),
                        pltpu.VMEM((hg, 2 * tile, LANES), F32)],
        compiler_params=_params("parallel", "parallel", "arbitrary"),
        name="attn_prompt",
    )(q, kt, v, lam_vec, g_sub)


def _kv_proj_t_kernel(x_ref, g_ref, gq_ref, wkt_ref, wv_ref, wq_ref, kt_ref, ktb_ref, v_ref, vb_ref, q_ref,
                      *, heads, q_scale):
    x = x_ref[...]
    xr = x * lax.rsqrt(jnp.mean(x * x, axis=-1, keepdims=True) + EPS)
    q = jnp.dot((xr * gq_ref[...]).astype(BF16), wq_ref[...], preferred_element_type=F32)
    q_ref[...] = (q * q_scale).astype(BF16)
    xn = (xr * g_ref[...]).astype(BF16)
    kt = _dot_nt(wkt_ref[...], xn)
    kt_ref[0] = kt
    ktb_ref[0, 0] = kt.astype(BF16)
    v = jnp.dot(xn, wv_ref[...], preferred_element_type=F32)
    v_ref[...] = v
    for h in range(heads):
        vb_ref[0, h] = v[:, h * LANES:(h + 1) * LANES].astype(BF16)


def _kv_proj_t(x, g_kv, g_q, wkt, wv, wq, q_scale, bsz, tm):
    rows, d = x.shape
    seqlen = rows // bsz
    width = wv.shape[1]
    heads = width // LANES
    assert seqlen % tm == 0
    nt = seqlen // tm
    row_blk = lambda b, i: (b * nt + i, 0)
    const = lambda b, i: (0, 0)
    return pl.pallas_call(
        functools.partial(_kv_proj_t_kernel, heads=heads, q_scale=q_scale),
        grid=(bsz, nt),
        in_specs=[pl.BlockSpec((tm, d), row_blk),
                  pl.BlockSpec((1, d), const),
                  pl.BlockSpec((1, d), const),
                  pl.BlockSpec((width, d), const),
                  pl.BlockSpec((d, width), const),
                  pl.BlockSpec((d, width), const)],
        out_specs=[pl.BlockSpec((1, width, tm), lambda b, i: (b, 0, i)),
                   pl.BlockSpec((1, 1, width, tm), lambda b, i: (b, i, 0, 0)),
                   pl.BlockSpec((tm, width), row_blk),
                   pl.BlockSpec((1, heads, tm, LANES), lambda b, i: (b, 0, i, 0)),
                   pl.BlockSpec((tm, width), row_blk)],
        out_shape=[jax.ShapeDtypeStruct((bsz, width, seqlen), F32),
                   jax.ShapeDtypeStruct((bsz, nt, width, tm), BF16),
                   jax.ShapeDtypeStruct((rows, width), F32),
                   jax.ShapeDtypeStruct((bsz, heads, seqlen, LANES), BF16),
                   jax.ShapeDtypeStruct((rows, width), BF16)],
        compiler_params=_params("parallel", "parallel"),
        name="kv_proj_t",
    )(x, g_kv, g_q, wkt, wv, wq)


def _attn_sample_kernel(q_ref, ckt_ref, cv_ref, kn_ref, vn_ref, lam_ref, gsub_ref, o_ref, qs_sc, m_sc, l_sc, acc_sc,
                        *, lq, heads, past, lam_init):
    j = pl.program_id(1)

    @pl.when(j == 0)
    def _():
        for h in range(heads):
            qs_sc[h] = _stack_maps(q_ref[0, :, h * LANES:(h + 1) * LANES])
        m_sc[...] = jnp.full(m_sc.shape, NEG_INF, F32)
        l_sc[...] = jnp.zeros_like(l_sc)
        acc_sc[...] = jnp.zeros_like(acc_sc)

    hr = range(heads)

    def update(ss, vts):
        m_prev = [m_sc[h] for h in hr]
        m_new = [jnp.maximum(m_prev[h], jnp.max(ss[h], axis=1, keepdims=True)) for h in hr]
        alpha = [jnp.exp2(m_prev[h] - m_new[h]) for h in hr]
        ps = [jnp.exp2(ss[h] - m_new[h]) for h in hr]
        pv = [jnp.dot(ps[h].astype(BF16), vts[h], preferred_element_type=F32) for h in hr]
        for h in hr:
            l_sc[h] = alpha[h] * l_sc[h] + jnp.sum(ps[h], axis=1, keepdims=True)
            acc_sc[h] = alpha[h] * acc_sc[h] + pv[h]
            m_sc[h] = m_new[h]

    v_heads = pltpu.einshape("phd->hpd", cv_ref[0])
    update([jnp.dot(qs_sc[h], ckt_ref[0, h * LANES:(h + 1) * LANES, :].astype(BF16), preferred_element_type=F32)
            for h in hr],
           [v_heads[h].astype(BF16) for h in hr])

    @pl.when(j == pl.num_programs(1) - 1)
    def _():
        row = lax.broadcasted_iota(jnp.int32, (2 * lq, lq), 0)
        row = jnp.where(row >= lq, row - lq, row)
        col = lax.broadcasted_iota(jnp.int32, (2 * lq, lq), 1)
        mask = ((past + col) // CHUNK) <= ((past + row) // CHUNK)
        lam = _lambda(lam_ref, lam_init)
        hs = [slice(h * LANES, (h + 1) * LANES) for h in hr]
        update([jnp.where(mask, _dot_nt(qs_sc[h], kn_ref[0, :, hs[h]]), NEG_INF) for h in hr],
               [vn_ref[0, :, hs[h]] for h in hr])
        for h in hr:
            o = _finish_head(acc_sc[h], l_sc[h], lam, gsub_ref[:, hs[h]], lam_init, lq)
            o_ref[0, :, hs[h]] = o.astype(BF16)


def _attn_sample(q, cache_kt, cache_v, k_new, v_new, lam_vec, g_sub, lam_init, tkc):
    bsz, lq, width = q.shape
    past = cache_kt.shape[2]
    heads = width // LANES
    assert past % tkc == 0 and past % CHUNK == 0
    new_blk = pl.BlockSpec((1, lq, width), lambda b, j: (b, 0, 0))
    return pl.pallas_call(
        functools.partial(_attn_sample_kernel, lq=lq, heads=heads, past=past, lam_init=lam_init),
        grid=(bsz, past // tkc),
        in_specs=[new_blk,
                  pl.BlockSpec((1, width, tkc), lambda b, j: (b, 0, j)),
                  pl.BlockSpec((1, tkc, heads, LANES), lambda b, j: (b, j, 0, 0)),
                  new_blk,
                  new_blk,
                  pl.BlockSpec((4, HD_B), lambda b, j: (0, 0)),
                  pl.BlockSpec((1, width), lambda b, j: (0, 0))],
        out_specs=new_blk,
        out_shape=jax.ShapeDtypeStruct((bsz, lq, width), BF16),
        scratch_shapes=[pltpu.VMEM((heads, 2 * lq, LANES), BF16),
                        pltpu.VMEM((heads, 2 * lq, 1), F32),
                        pltpu.VMEM((heads, 2 * lq, 1), F32),
                        pltpu.VMEM((heads, 2 * lq, LANES), F32)],
        compiler_params=_params("parallel", "arbitrary"),
        name="attn_sample",
    )(q, cache_kt, cache_v, k_new, v_new, lam_vec, g_sub)


def _lambda_init(layer_idx):
    return 0.8 - 0.6 * math.exp(-0.3 * layer_idx)


def _trunk(x3, state, cache, p, mlstm_blk):
    bsz, seqlen, d = x3.shape
    rows = bsz * seqlen
    tm = min(ROW_TILE, rows)
    x = x3.reshape(rows, d)
    row = lambda g: g.reshape(1, -1)

    qk_w = H_A * p["dk"]
    v_w = H_A * p["dv"]
    w_in = p["w_in_a"]
    w_parts = [w_in[:, 0:qk_w], w_in[:, qk_w:2 * qk_w], w_in[:, 2 * qk_w:2 * qk_w + v_w],
               w_in[:, 2 * qk_w + v_w:2 * qk_w + 2 * v_w]]
    w_gate = jnp.pad(w_in[:, 2 * qk_w + 2 * v_w:], ((0, 0), (0, LANES - 2 * H_A)))
    ws = [w.astype(BF16) for w in w_parts] + [w_gate.astype(BF16)]
    specs = [(1.0, (BF16,)), (p["dk"] ** -0.5, (BF16,)), (1.0, (BF16,)), (1.0, (BF16,)), (1.0, (F32,))]
    q, k, v, o, gates = _norm_proj(x, row(p["g_mix_pre"][0]), ws, specs, tm)

    bias = jnp.concatenate([p["b_i_a"], p["b_f_a"]]).astype(F32)
    b_col = bias.reshape(2 * H_A, 1)
    c0, n0, m0 = state
    m0b = jnp.broadcast_to(m0[:, :, None], (bsz, H_A, LANES))
    n0b = jnp.broadcast_to(n0[:, :, :, None], n0.shape + (LANES,))
    r3 = lambda a: a.reshape(bsz, seqlen, a.shape[-1])
    hg, c1, n1b, m1 = _mlstm(r3(q), r3(k), r3(v), r3(o), r3(gates), b_col, row(p["g_h_a"]), c0, n0b, m0b,
                             mlstm_blk, MLSTM_STREAMS)
    n1 = n1b[:, :, :, 0]
    x = _mix_mlp(hg.reshape(rows, v_w), x, p["w_out_a"].astype(BF16), row(p["g_mix_post"][0]),
                 row(p["g_ffn_pre"][0]), p["w_up"][0].astype(BF16), p["w_down"][0].astype(BF16),
                 row(p["g_ffn_post"][0]), tm, FF_TILE)

    kb_w = H_B * 2 * HD_B
    w_k = p["w_kv"][:, :kb_w]
    w_v = p["w_kv"][:, kb_w:].astype(BF16)
    q_scale = HD_B ** -0.5 * math.log2(math.e)
    w_q = p["w_q_b"].astype(BF16)
    g_q = row(p["g_mix_pre"][1])

    lam_init = _lambda_init(1)
    lam_vec = jnp.stack([p["lambda_q1"], p["lambda_k1"], p["lambda_q2"], p["lambda_k2"]]).astype(F32)
    g_sub = row(p["g_sub_b"])
    shape3 = (bsz, seqlen, kb_w)
    if cache is None:
        assert ATTN_TILE == tm
        kt_new, kt_bf, v_new, v_bf, qb = _kv_proj_t(x, row(p["g_kv"]), g_q, w_k.T.astype(BF16), w_v, w_q, q_scale,
                                                    bsz, ATTN_TILE)
        k_out = kt_new.reshape(bsz, H_B, 2, HD_B, seqlen).transpose(0, 4, 1, 2, 3)
        att = _attn_prompt(qb.reshape(shape3), kt_bf, v_bf, lam_vec, g_sub, lam_init, ATTN_HEAD_GROUP)
    else:
        (qb,) = _norm_proj(x, g_q, [w_q], [(q_scale, (BF16,))], tm)
        k_new, k_bf, v_new, v_bf = _norm_proj(x, row(p["g_kv"]), [w_k.astype(BF16), w_v],
                                              [(1.0, (F32, BF16)), (1.0, (F32, BF16))], tm)
        k_out = k_new.reshape(bsz, seqlen, H_B, 2, HD_B)
        ck, cv = cache
        past = ck.shape[1]
        ckt = ck.transpose(0, 2, 3, 4, 1).reshape(bsz, kb_w, past)
        att = _attn_sample(qb.reshape(shape3), ckt, cv, k_bf.reshape(shape3), v_bf.reshape(shape3),
                           lam_vec, g_sub, lam_init, min(SAMPLE_KEY_TILE, past))
    x = _mix_mlp(att.reshape(rows, kb_w), x, p["w_o_b"].astype(BF16), row(p["g_mix_post"][1]),
                 row(p["g_ffn_pre"][1]), p["w_up"][1].astype(BF16), p["w_down"][1].astype(BF16),
                 row(p["g_ffn_post"][1]), tm, FF_TILE)

    y = x.reshape(bsz, seqlen, d)
    v_out = v_new.reshape(bsz, seqlen, H_B, 2 * HD_B)
    return y, (c1[None], n1[None], m1[None, :, :, 0]), k_out, v_out


def kernel(x_prompt, x_sample, cache_k, cache_v, state_c, state_n, state_m, w_in_a, b_i_a, b_f_a, g_h_a, w_out_a,
           g_kv, w_kv, w_q_b, lambda_q1, lambda_k1, lambda_q2, lambda_k2, g_sub_b, w_o_b, g_mix_pre, g_mix_post,
           g_ffn_pre, g_ffn_post, w_up, w_down):
    assert w_in_a.shape[0] == 1 and w_q_b.shape[0] == 1 and g_mix_pre.shape[0] == 2
    dk, dv = state_c.shape[3], state_c.shape[4]
    p = {"dk": dk, "dv": dv,
         "w_in_a": w_in_a[0], "b_i_a": b_i_a[0], "b_f_a": b_f_a[0], "g_h_a": g_h_a[0], "w_out_a": w_out_a[0],
         "g_kv": g_kv, "w_kv": w_kv, "w_q_b": w_q_b[0],
         "lambda_q1": lambda_q1[0], "lambda_k1": lambda_k1[0], "lambda_q2": lambda_q2[0], "lambda_k2": lambda_k2[0],
         "g_sub_b": g_sub_b[0], "w_o_b": w_o_b[0], "g_mix_pre": g_mix_pre, "g_mix_post": g_mix_post,
         "g_ffn_pre": g_ffn_pre, "g_ffn_post": g_ffn_post, "w_up": w_up, "w_down": w_down}
    bsz = x_prompt.shape[0]
    zero_state = (jnp.zeros((bsz, H_A, dk, dv), F32), jnp.zeros((bsz, H_A, dk), F32), jnp.zeros((bsz, H_A), F32))
    y_p, (c_p, n_p, m_p), k_p, v_p = _trunk(x_prompt, zero_state, None, p, MLSTM_BLOCK)
    y_s, (c_s, n_s, m_s), k_s, v_s = _trunk(
        x_sample, (state_c[0].astype(F32), state_n[0].astype(F32), state_m[0].astype(F32)),
        (cache_k, cache_v), p, x_sample.shape[1])
    return (y_p, y_s, c_p, n_p, m_p, k_p, v_p, c_s, n_s, m_s, k_s, v_s)
```

```python
import functools
import math

import jax
import jax.numpy as jnp
from jax import lax
from jax.experimental import pallas as pl
from jax.experimental.pallas import tpu as pltpu

F32 = jnp.float32
BF16 = jnp.bfloat16

EPS = 1e-6
NEG_INF = -1e30
CHUNK = 64
LANES = 128
H_A = 8
H_B = 8
HD_B = 64
MLSTM_BLOCK = 128
MLSTM_HEAD_GROUP = 8
MLSTM_STREAMS = 1
ATTN_TILE = 512
KEY_LOOP_UNROLL = 2
ATTN_HEAD_GROUP = 2
ROW_TILE = 512
FF_TILE = 1024
SAMPLE_KEY_TILE = 512
VMEM_LIMIT_BYTES = 48 * 1024 * 1024
MLP_ROW_TILE = 1024
MLP_VMEM_LIMIT_BYTES = 58 * 1024 * 1024


def _params(*sem, vmem_limit_bytes=VMEM_LIMIT_BYTES):
    return pltpu.CompilerParams(dimension_semantics=sem, vmem_limit_bytes=vmem_limit_bytes)


def _rms(x, g):
    return x * lax.rsqrt(jnp.mean(x * x, axis=-1, keepdims=True) + EPS) * g


def _log_sigmoid(x):
    return jnp.minimum(x, 0.0) - jnp.log1p(jnp.exp(-jnp.abs(x)))


def _dot_nt(a, b):
    return lax.dot_general(a, b, (((1,), (1,)), ((), ())), preferred_element_type=F32)


def _dot_tn(a, b):
    return lax.dot_general(a, b, (((0,), (0,)), ((), ())), preferred_element_type=F32)


def _norm_proj_kernel(x_ref, g_ref, *refs, specs, parts):
    n_w = len(specs)
    w_refs, o_refs = refs[:n_w], refs[n_w:]
    rows = x_ref.shape[0] // parts
    rs = [slice(r * rows, (r + 1) * rows) for r in range(parts)]
    xn = [_rms(x_ref[s, :], g_ref[...]).astype(BF16) for s in rs]
    oi = 0
    for w_ref, (scale, dtypes) in zip(w_refs, specs):
        for r in range(parts):
            acc = jnp.dot(xn[r], w_ref[...], preferred_element_type=F32)
            if scale != 1.0:
                acc = acc * scale
            for k, dt in enumerate(dtypes):
                o_refs[oi + k][rs[r], :] = acc.astype(dt)
        oi += len(dtypes)


def _norm_proj(x, g, ws, specs, tm):
    m, d = x.shape
    assert m % tm == 0 and (tm <= ROW_TILE or tm % ROW_TILE == 0)
    parts = max(1, tm // ROW_TILE)
    out_shapes, out_specs = [], []
    for w, (_, dtypes) in zip(ws, specs):
        for dt in dtypes:
            out_shapes.append(jax.ShapeDtypeStruct((m, w.shape[1]), dt))
            out_specs.append(pl.BlockSpec((tm, w.shape[1]), lambda i: (i, 0)))
    in_specs = [pl.BlockSpec((tm, d), lambda i: (i, 0)), pl.BlockSpec((1, d), lambda i: (0, 0))]
    in_specs += [pl.BlockSpec(w.shape, lambda i: (0, 0)) for w in ws]
    return pl.pallas_call(
        functools.partial(_norm_proj_kernel, specs=tuple(specs), parts=parts),
        grid=(m // tm,),
        in_specs=in_specs,
        out_specs=out_specs,
        out_shape=out_shapes,
        compiler_params=_params("parallel"),
        name="norm_proj",
    )(x, g, *ws)


def _mix_mlp_kernel(h_ref, x_ref, wo_ref, gmix_ref, gpre_ref, wup_ref, wdn_ref, gpost_ref, o_ref, *, tf, parts):
    rows = h_ref.shape[0] // parts
    rs = [slice(r * rows, (r + 1) * rows) for r in range(parts)]
    x1 = [x_ref[s, :] + _rms(jnp.dot(h_ref[s, :], wo_ref[...], preferred_element_type=F32), gmix_ref[...])
          for s in rs]
    xn = [_rms(x, gpre_ref[...]).astype(BF16) for x in x1]
    acc = [None] * parts
    for c in range(wup_ref.shape[1] // tf):
        for r in range(parts):
            up = jnp.maximum(jnp.dot(xn[r], wup_ref[:, c * tf:(c + 1) * tf], preferred_element_type=F32), 0.0)
            part = jnp.dot((up * up).astype(BF16), wdn_ref[c * tf:(c + 1) * tf, :], preferred_element_type=F32)
            acc[r] = part if acc[r] is None else acc[r] + part
    for r in range(parts):
        o_ref[rs[r], :] = x1[r] + _rms(acc[r], gpost_ref[...])


def _mix_mlp(h, x, w_o, g_mix, g_pre, w_up, w_down, g_post, tm, tf):
    m, d = x.shape
    k = h.shape[1]
    ff = w_up.shape[1]
    assert m % tm == 0 and ff % tf == 0 and (tm <= ROW_TILE or tm % ROW_TILE == 0)
    resident = lambda shape: pl.BlockSpec(shape, lambda i: (0, 0), pipeline_mode=pl.Buffered(1))
    return pl.pallas_call(
        functools.partial(_mix_mlp_kernel, tf=tf, parts=max(1, tm // ROW_TILE)),
        grid=(m // tm,),
        in_specs=[pl.BlockSpec((tm, k), lambda i: (i, 0)),
                  pl.BlockSpec((tm, d), lambda i: (i, 0)),
                  resident((k, d)),
                  resident((1, d)),
                  resident((1, d)),
                  resident((d, ff)),
                  resident((ff, d)),
                  resident((1, d))],
        out_specs=pl.BlockSpec((tm, d), lambda i: (i, 0)),
        out_shape=jax.ShapeDtypeStruct((m, d), F32),
        compiler_params=_params("parallel", vmem_limit_bytes=MLP_VMEM_LIMIT_BYTES),
        name="mix_mlp",
    )(h, x, w_o, g_mix, g_pre, w_up, w_down, g_post)


def _mlstm_kernel(q_ref, k_ref, v_ref, o_ref, gate_ref, bcol_ref, gh_ref, c0_ref, n0_ref, m0_ref,
                  hg_ref, c_ref, n_ref, m_ref, *, nb, blk, heads, dk, dv):
    t = pl.program_id(1)

    @pl.when(t == 0)
    def _():
        c_ref[...] = c0_ref[...]
        n_ref[...] = n0_ref[...]
        m_ref[...] = m0_ref[...]

    row = lax.broadcasted_iota(jnp.int32, (blk, blk), 0)
    col = lax.broadcasted_iota(jnp.int32, (blk, blk), 1)
    causal = row >= col
    triu = (row <= col).astype(F32)

    def square(a):
        if a.shape[1] < LANES:
            a = jnp.concatenate([a, jnp.zeros((a.shape[0], LANES - a.shape[1]), F32)], axis=1)
        if a.shape[0] < LANES:
            a = jnp.concatenate([a, jnp.zeros((LANES - a.shape[0], a.shape[1]), F32)], axis=0)
        return a

    all_pairs = [(b, h) for b in range(nb) for h in range(heads)]
    for g0 in range(0, len(all_pairs), MLSTM_HEAD_GROUP):
        _mlstm_group(all_pairs[g0:g0 + MLSTM_HEAD_GROUP], q_ref, k_ref, v_ref, o_ref, gate_ref, bcol_ref, gh_ref,
                     hg_ref, c_ref, n_ref, m_ref, causal, triu, square, blk=blk, heads=heads, dk=dk, dv=dv)


def _mlstm_group(pairs, q_ref, k_ref, v_ref, o_ref, gate_ref, bcol_ref, gh_ref, hg_ref, c_ref, n_ref, m_ref,
                 causal, triu, square, *, blk, heads, dk, dv):
    hr = range(len(pairs))
    streams = sorted({b for b, _ in pairs})
    qs = [q_ref[b, :, h * dk:(h + 1) * dk] for b, h in pairs]
    ks = [k_ref[b, :, h * dk:(h + 1) * dk] for b, h in pairs]
    vs = [v_ref[b, :, h * dv:(h + 1) * dv] for b, h in pairs]
    c0s = [c_ref[b, h] for b, h in pairs]
    n0s = [n_ref[b, h] for b, h in pairs]
    m0s = [m_ref[b, h:h + 1, :] for b, h in pairs]
    ones = jnp.ones((blk, LANES), BF16)
    v1s = [jnp.concatenate([vs[h], ones], axis=1) for h in hr]
    cns = [jnp.concatenate([c0s[h], n0s[h]], axis=1).astype(BF16) for h in hr]

    qk = [_dot_nt(qs[h], ks[h]) for h in hr]
    qcn = [jnp.dot(qs[h], cns[h], preferred_element_type=F32) for h in hr]

    stats_cols, r_rows = {}, {}
    for b in streams:
        gates_row = square(gate_ref[b]).T[:, :blk]
        z_row = gates_row[0:2 * heads, :] + bcol_ref[...]
        li_row = z_row[0:heads, :]
        lf_row = _log_sigmoid(z_row[heads:2 * heads, :])
        fcum_row = jnp.dot(lf_row, triu, preferred_element_type=F32, precision=lax.Precision.HIGHEST)
        r_rows[b] = li_row - fcum_row
        stats_cols[b] = square(jnp.concatenate([li_row, fcum_row], axis=0)).T[:blk, :]
    r_row = [r_rows[b][h:h + 1, :] for b, h in pairs]
    fcs = [jnp.broadcast_to(stats_cols[b][:, heads + h:heads + h + 1], (blk, LANES)) for b, h in pairs]
    lis = [jnp.broadcast_to(stats_cols[b][:, h:h + 1], (blk, LANES)) for b, h in pairs]

    def wide(x):
        return x[:, :blk] if blk <= LANES else jnp.concatenate([x] * (blk // LANES), axis=1)

    dmat = [jnp.where(causal, wide(fcs[h]) + r_row[h], NEG_INF) for h in hr]
    a = [fcs[h] + m0s[h] for h in hr]
    m_new = [jnp.maximum(a[h], jnp.max(dmat[h], axis=1, keepdims=True)) for h in hr]
    s = [qk[h] * jnp.exp(dmat[h] - wide(m_new[h])) for h in hr]
    inter = [jnp.exp(a[h] - m_new[h]) for h in hr]
    sv1 = [jnp.dot(s[h].astype(BF16), v1s[h], preferred_element_type=F32) for h in hr]
    qn = [sv1[h][:, dv:] + inter[h] * qcn[h][:, dv:] for h in hr]
    hh = [(sv1[h][:, :dv] + inter[h] * qcn[h][:, :dv]) / jnp.maximum(jnp.abs(qn[h]), jnp.exp(-m_new[h]))
          for h in hr]

    for i, (b, h) in enumerate(pairs):
        hn = _rms(hh[i], gh_ref[:, h * dv:(h + 1) * dv])
        og = o_ref[b, :, h * dv:(h + 1) * dv].astype(F32)
        hg_ref[b, :, h * dv:(h + 1) * dv] = (hn * jax.nn.sigmoid(og)).astype(BF16)

    m_last = [m_new[h][blk - 1:blk, :] for h in hr]
    w_last = [jnp.exp(fcs[h][blk - 1:blk, :] - fcs[h] + lis[h] - m_last[h]) for h in hr]
    decay = [jnp.exp(a[h][blk - 1:blk, :] - m_last[h]) for h in hr]
    kw = [ks[h].astype(F32) * w_last[h][:, :dk] for h in hr]
    kv1 = [_dot_tn(kw[h].astype(BF16), v1s[h]) for h in hr]
    for i, (b, h) in enumerate(pairs):
        c_ref[b, h] = decay[i] * c0s[i] + kv1[i][:, :dv]
        n_ref[b, h] = decay[i] * n0s[i] + kv1[i][:, dv:]
        m_ref[b, h:h + 1, :] = m_last[i]


def _mlstm(q, k, v, o, gates, b_col, g_h, c0, n0, m0, blk, nb):
    bsz, seqlen, _ = q.shape
    heads, dk, dv = c0.shape[1], c0.shape[2], c0.shape[3]
    assert seqlen % blk == 0 and bsz % nb == 0 and dv == LANES
    row_map = lambda b, t: (b, t, 0)
    const2 = lambda b, t: (0, 0)
    st4 = lambda b, t: (b, 0, 0, 0)
    st3 = lambda b, t: (b, 0, 0)
    return pl.pallas_call(
        functools.partial(_mlstm_kernel, nb=nb, blk=blk, heads=heads, dk=dk, dv=dv),
        grid=(bsz // nb, seqlen // blk),
        in_specs=[pl.BlockSpec((nb, blk, heads * dk), row_map),
                  pl.BlockSpec((nb, blk, heads * dk), row_map),
                  pl.BlockSpec((nb, blk, heads * dv), row_map),
                  pl.BlockSpec((nb, blk, heads * dv), row_map),
                  pl.BlockSpec((nb, blk, LANES), row_map),
                  pl.BlockSpec((2 * heads, 1), const2),
                  pl.BlockSpec((1, heads * dv), const2),
                  pl.BlockSpec((nb, heads, dk, dv), st4),
                  pl.BlockSpec((nb, heads, dk, LANES), st4),
                  pl.BlockSpec((nb, heads, LANES), st3)],
        out_specs=[pl.BlockSpec((nb, blk, heads * dv), row_map),
                   pl.BlockSpec((nb, heads, dk, dv), st4),
                   pl.BlockSpec((nb, heads, dk, LANES), st4),
                   pl.BlockSpec((nb, heads, LANES), st3)],
        out_shape=[jax.ShapeDtypeStruct((bsz, seqlen, heads * dv), BF16),
                   jax.ShapeDtypeStruct((bsz, heads, dk, dv), F32),
                   jax.ShapeDtypeStruct((bsz, heads, dk, LANES), F32),
                   jax.ShapeDtypeStruct((bsz, heads, LANES), F32)],
        compiler_params=_params("parallel", "arbitrary"),
        name="mlstm",
    )(q, k, v, o, gates, b_col, g_h, c0, n0, m0)


def _stack_maps(q):
    lane = lax.broadcasted_iota(jnp.int32, q.shape, 1)
    zero = jnp.zeros_like(q)
    return jnp.concatenate([jnp.where(lane < HD_B, q, zero), jnp.where(lane >= HD_B, q, zero)], axis=0)


def _lambda(lam_ref, lam_init):
    lv = lam_ref[...]
    a = jnp.sum(lv[0:1] * lv[1:2], axis=1, keepdims=True)
    b = jnp.sum(lv[2:3] * lv[3:4], axis=1, keepdims=True)
    return jnp.exp(a) - jnp.exp(b) + lam_init


def _finish_head(acc, l, lam, g_sub, lam_init, tq):
    w = acc / l
    o = w[0:tq] - lam * w[tq:2 * tq]
    return _rms(o, g_sub) * (1.0 - lam_init)


def _attn_prompt_kernel(q_ref, kt_ref, v_ref, lam_ref, gsub_ref, o_ref, qs_sc, m_sc, l_sc, acc_sc,
                        *, tile, hg, lam_init):
    i = pl.program_id(2)
    heads = range(hg)
    for g in heads:
        qs_sc[g] = _stack_maps(q_ref[0, :, g * LANES:(g + 1) * LANES])
    m_sc[...] = jnp.full(m_sc.shape, NEG_INF, F32)
    l_sc[...] = jnp.zeros_like(l_sc)
    acc_sc[...] = jnp.zeros_like(acc_sc)
    n_lane_tiles = tile // LANES

    def step(j, masked):
        ss = [jnp.dot(qs_sc[g], kt_ref[0, j, g * LANES:(g + 1) * LANES, :], preferred_element_type=F32)
              for g in heads]
        if masked:
            row = lax.broadcasted_iota(jnp.int32, ss[0].shape, 0) & (tile - 1)
            col = lax.broadcasted_iota(jnp.int32, ss[0].shape, 1)
            ss = [jnp.where((col // CHUNK) <= (row // CHUNK), s, NEG_INF) for s in ss]
        start = pl.multiple_of(j * tile, tile)
        m_prev, m_new, alpha, ps, pv = [], [], [], [], []
        for g in heads:
            m_prev.append(m_sc[g])
            m_new.append(jnp.maximum(m_prev[g], jnp.max(ss[g], axis=1, keepdims=True)))
            alpha.append(jnp.exp2(m_prev[g] - m_new[g]))
            ps.append(jnp.exp2((ss[g] - jnp.concatenate([m_new[g]] * n_lane_tiles, axis=1)).astype(BF16)))
            pv.append(jnp.dot(ps[g], v_ref[0, g, pl.ds(start, tile), :], preferred_element_type=F32))
        for g in heads:
            p_sum = ps[g][:, 0:LANES]
            for c in range(1, n_lane_tiles):
                p_sum = p_sum + ps[g][:, c * LANES:(c + 1) * LANES]
            l_sc[g] = alpha[g] * l_sc[g] + p_sum.astype(F32)
            acc_sc[g] = alpha[g] * acc_sc[g] + pv[g]
            m_sc[g] = m_new[g]

    def body(jj, carry):
        for u in range(KEY_LOOP_UNROLL):
            step(KEY_LOOP_UNROLL * jj + u, False)
        return carry

    n_full = i // KEY_LOOP_UNROLL
    lax.fori_loop(0, n_full, body, 0)
    for rem in range(1, KEY_LOOP_UNROLL):
        @pl.when(i - KEY_LOOP_UNROLL * n_full >= rem)
        def _():
            step(KEY_LOOP_UNROLL * n_full + rem - 1, False)
    step(i, True)
    lam = _lambda(lam_ref, lam_init)
    for g in heads:
        l = jnp.sum(l_sc[g], axis=1, keepdims=True)
        o = _finish_head(acc_sc[g], l, lam, gsub_ref[:, g * LANES:(g + 1) * LANES], lam_init, tile)
        o_ref[0, :, g * LANES:(g + 1) * LANES] = o.astype(BF16)


def _attn_prompt(q, kt, v, lam_vec, g_sub, lam_init, hg):
    bsz, seqlen, width = q.shape
    tile = kt.shape[3]
    heads = width // LANES
    gw = hg * LANES
    assert seqlen % tile == 0 and tile % CHUNK == 0 and (tile & (tile - 1)) == 0 and tile % LANES == 0
    assert heads % hg == 0
    return pl.pallas_call(
        functools.partial(_attn_prompt_kernel, tile=tile, hg=hg, lam_init=lam_init),
        grid=(bsz, heads // hg, seqlen // tile),
        in_specs=[pl.BlockSpec((1, tile, gw), lambda b, h, i: (b, i, h)),
                  pl.BlockSpec((1, seqlen // tile, gw, tile), lambda b, h, i: (b, 0, h, 0)),
                  pl.BlockSpec((1, hg, seqlen, LANES), lambda b, h, i: (b, h, 0, 0)),
                  pl.BlockSpec((4, HD_B), lambda b, h, i: (0, 0)),
                  pl.BlockSpec((1, gw), lambda b, h, i: (0, h))],
        out_specs=pl.BlockSpec((1, tile, gw), lambda b, h, i: (b, i, h)),
        out_shape=jax.ShapeDtypeStruct((bsz, seqlen, width), BF16),
        scratch_shapes=[pltpu.VMEM((hg, 2 * tile, LANES), BF16),
                        pltpu.VMEM((hg, 2 * tile, LANES), F32),
                        pltpu.VMEM((hg, 2 * tile, LANES), F32),
                        pltpu.VMEM((hg, 2 * tile, LANES), F32)],
        compiler_params=_params("parallel", "parallel", "arbitrary"),
        name="attn_prompt",
    )(q, kt, v, lam_vec, g_sub)


def _kv_proj_t_kernel(x_ref, g_ref, gq_ref, wkt_ref, wv_ref, wq_ref, kt_ref, ktb_ref, v_ref, vb_ref, q_ref,
                      *, heads, q_scale, parts):
    rows = x_ref.shape[0] // parts
    rs = [slice(r * rows, (r + 1) * rows) for r in range(parts)]
    xs = [x_ref[s, :] for s in rs]
    xr = [x * lax.rsqrt(jnp.mean(x * x, axis=-1, keepdims=True) + EPS) for x in xs]
    xq = [(x * gq_ref[...]).astype(BF16) for x in xr]
    xn = [(x * g_ref[...]).astype(BF16) for x in xr]
    for r in range(parts):
        q = jnp.dot(xq[r], wq_ref[...], preferred_element_type=F32)
        q_ref[rs[r], :] = (q * q_scale).astype(BF16)
    for r in range(parts):
        kt = _dot_nt(wkt_ref[...], xn[r])
        kt_ref[0, :, rs[r]] = kt
        ktb_ref[0, r] = kt.astype(BF16)
    for r in range(parts):
        v = jnp.dot(xn[r], wv_ref[...], preferred_element_type=F32)
        v_ref[rs[r], :] = v
        for h in range(heads):
            vb_ref[0, h, rs[r], :] = v[:, h * LANES:(h + 1) * LANES].astype(BF16)


def _kv_proj_t(x, g_kv, g_q, wkt, wv, wq, q_scale, bsz, tm, kt_tile):
    rows, d = x.shape
    seqlen = rows // bsz
    width = wv.shape[1]
    heads = width // LANES
    assert seqlen % tm == 0 and tm % kt_tile == 0
    nt = seqlen // tm
    parts = tm // kt_tile
    row_blk = lambda b, i: (b * nt + i, 0)
    resident = lambda shape: pl.BlockSpec(shape, lambda b, i: (0, 0), pipeline_mode=pl.Buffered(1))
    return pl.pallas_call(
        functools.partial(_kv_proj_t_kernel, heads=heads, q_scale=q_scale, parts=parts),
        grid=(bsz, nt),
        in_specs=[pl.BlockSpec((tm, d), row_blk),
                  resident((1, d)),
                  resident((1, d)),
                  resident((width, d)),
                  resident((d, width)),
                  resident((d, width))],
        out_specs=[pl.BlockSpec((1, width, tm), lambda b, i: (b, 0, i)),
                   pl.BlockSpec((1, parts, width, kt_tile), lambda b, i: (b, i, 0, 0)),
                   pl.BlockSpec((tm, width), row_blk),
                   pl.BlockSpec((1, heads, tm, LANES), lambda b, i: (b, 0, i, 0)),
                   pl.BlockSpec((tm, width), row_blk)],
        out_shape=[jax.ShapeDtypeStruct((bsz, width, seqlen), F32),
                   jax.ShapeDtypeStruct((bsz, seqlen // kt_tile, width, kt_tile), BF16),
                   jax.ShapeDtypeStruct((rows, width), F32),
                   jax.ShapeDtypeStruct((bsz, heads, seqlen, LANES), BF16),
                   jax.ShapeDtypeStruct((rows, width), BF16)],
        compiler_params=_params("parallel", "parallel", vmem_limit_bytes=MLP_VMEM_LIMIT_BYTES),
        name="kv_proj_t",
    )(x, g_kv, g_q, wkt, wv, wq)


def _attn_sample_kernel(q_ref, ckt_ref, cv_ref, kn_ref, vn_ref, lam_ref, gsub_ref, o_ref, qs_sc, m_sc, l_sc, acc_sc,
                        *, lq, heads, past, lam_init):
    j = pl.program_id(1)

    @pl.when(j == 0)
    def _():
        for h in range(heads):
            qs_sc[h] = _stack_maps(q_ref[0, :, h * LANES:(h + 1) * LANES])
        m_sc[...] = jnp.full(m_sc.shape, NEG_INF, F32)
        l_sc[...] = jnp.zeros_like(l_sc)
        acc_sc[...] = jnp.zeros_like(acc_sc)

    hr = range(heads)

    def update(ss, vts):
        m_prev = [m_sc[h] for h in hr]
        m_new = [jnp.maximum(m_prev[h], jnp.max(ss[h], axis=1, keepdims=True)) for h in hr]
        alpha = [jnp.exp2(m_prev[h] - m_new[h]) for h in hr]
        ps = [jnp.exp2(ss[h] - m_new[h]) for h in hr]
        pv = [jnp.dot(ps[h].astype(BF16), vts[h], preferred_element_type=F32) for h in hr]
        for h in hr:
            l_sc[h] = alpha[h] * l_sc[h] + jnp.sum(ps[h], axis=1, keepdims=True)
            acc_sc[h] = alpha[h] * acc_sc[h] + pv[h]
            m_sc[h] = m_new[h]

    v_heads = pltpu.einshape("phd->hpd", cv_ref[0])
    update([jnp.dot(qs_sc[h], ckt_ref[0, h * LANES:(h + 1) * LANES, :].astype(BF16), preferred_element_type=F32)
            for h in hr],
           [v_heads[h].astype(BF16) for h in hr])

    @pl.when(j == pl.num_programs(1) - 1)
    def _():
        row = lax.broadcasted_iota(jnp.int32, (2 * lq, lq), 0)
        row = jnp.where(row >= lq, row - lq, row)
        col = lax.broadcasted_iota(jnp.int32, (2 * lq, lq), 1)
        mask = ((past + col) // CHUNK) <= ((past + row) // CHUNK)
        lam = _lambda(lam_ref, lam_init)
        hs = [slice(h * LANES, (h + 1) * LANES) for h in hr]
        update([jnp.where(mask, _dot_nt(qs_sc[h], kn_ref[0, :, hs[h]]), NEG_INF) for h in hr],
               [vn_ref[0, :, hs[h]] for h in hr])
        for h in hr:
            o = _finish_head(acc_sc[h], l_sc[h], lam, gsub_ref[:, hs[h]], lam_init, lq)
            o_ref[0, :, hs[h]] = o.astype(BF16)


def _attn_sample(q, cache_kt, cache_v, k_new, v_new, lam_vec, g_sub, lam_init, tkc):
    bsz, lq, width = q.shape
    past = cache_kt.shape[2]
    heads = width // LANES
    assert past % tkc == 0 and past % CHUNK == 0
    new_blk = pl.BlockSpec((1, lq, width), lambda b, j: (b, 0, 0))
    return pl.pallas_call(
        functools.partial(_attn_sample_kernel, lq=lq, heads=heads, past=past, lam_init=lam_init),
        grid=(bsz, past // tkc),
        in_specs=[new_blk,
                  pl.BlockSpec((1, width, tkc), lambda b, j: (b, 0, j)),
                  pl.BlockSpec((1, tkc, heads, LANES), lambda b, j: (b, j, 0, 0)),
                  new_blk,
                  new_blk,
                  pl.BlockSpec((4, HD_B), lambda b, j: (0, 0)),
                  pl.BlockSpec((1, width), lambda b, j: (0, 0))],
        out_specs=new_blk,
        out_shape=jax.ShapeDtypeStruct((bsz, lq, width), BF16),
        scratch_shapes=[pltpu.VMEM((heads, 2 * lq, LANES), BF16),
                        pltpu.VMEM((heads, 2 * lq, 1), F32),
                        pltpu.VMEM((heads, 2 * lq, 1), F32),
                        pltpu.VMEM((heads, 2 * lq, LANES), F32)],
        compiler_params=_params("parallel", "arbitrary"),
        name="attn_sample",
    )(q, cache_kt, cache_v, k_new, v_new, lam_vec, g_sub)


def _lambda_init(layer_idx):
    return 0.8 - 0.6 * math.exp(-0.3 * layer_idx)


def _trunk(x3, state, cache, p, mlstm_blk):
    bsz, seqlen, d = x3.shape
    rows = bsz * seqlen
    tm = min(ROW_TILE, rows)
    x = x3.reshape(rows, d)
    row = lambda g: g.reshape(1, -1)

    qk_w = H_A * p["dk"]
    v_w = H_A * p["dv"]
    w_in = p["w_in_a"]
    w_parts = [w_in[:, 0:qk_w], w_in[:, qk_w:2 * qk_w], w_in[:, 2 * qk_w:2 * qk_w + v_w],
               w_in[:, 2 * qk_w + v_w:2 * qk_w + 2 * v_w]]
    w_gate = jnp.pad(w_in[:, 2 * qk_w + 2 * v_w:], ((0, 0), (0, LANES - 2 * H_A)))
    ws = [w.astype(BF16) for w in w_parts] + [w_gate.astype(BF16)]
    specs = [(1.0, (BF16,)), (p["dk"] ** -0.5, (BF16,)), (1.0, (BF16,)), (1.0, (BF16,)), (1.0, (F32,))]
    q, k, v, o, gates = _norm_proj(x, row(p["g_mix_pre"][0]), ws, specs, min(MLP_ROW_TILE, rows))

    bias = jnp.concatenate([p["b_i_a"], p["b_f_a"]]).astype(F32)
    b_col = bias.reshape(2 * H_A, 1)
    c0, n0, m0 = state
    m0b = jnp.broadcast_to(m0[:, :, None], (bsz, H_A, LANES))
    n0b = jnp.broadcast_to(n0[:, :, :, None], n0.shape + (LANES,))
    r3 = lambda a: a.reshape(bsz, seqlen, a.shape[-1])
    hg, c1, n1b, m1 = _mlstm(r3(q), r3(k), r3(v), r3(o), r3(gates), b_col, row(p["g_h_a"]), c0, n0b, m0b,
                             mlstm_blk, MLSTM_STREAMS)
    n1 = n1b[:, :, :, 0]
    x = _mix_mlp(hg.reshape(rows, v_w), x, p["w_out_a"].astype(BF16), row(p["g_mix_post"][0]),
                 row(p["g_ffn_pre"][0]), p["w_up"][0].astype(BF16), p["w_down"][0].astype(BF16),
                 row(p["g_ffn_post"][0]), min(MLP_ROW_TILE, rows), FF_TILE)

    kb_w = H_B * 2 * HD_B
    w_k = p["w_kv"][:, :kb_w]
    w_v = p["w_kv"][:, kb_w:].astype(BF16)
    q_scale = HD_B ** -0.5 * math.log2(math.e)
    w_q = p["w_q_b"].astype(BF16)
    g_q = row(p["g_mix_pre"][1])

    lam_init = _lambda_init(1)
    lam_vec = jnp.stack([p["lambda_q1"], p["lambda_k1"], p["lambda_q2"], p["lambda_k2"]]).astype(F32)
    g_sub = row(p["g_sub_b"])
    shape3 = (bsz, seqlen, kb_w)
    if cache is None:
        kt_new, kt_bf, v_new, v_bf, qb = _kv_proj_t(x, row(p["g_kv"]), g_q, w_k.T.astype(BF16), w_v, w_q, q_scale,
                                                    bsz, min(MLP_ROW_TILE, seqlen), ATTN_TILE)
        k_out = kt_new.reshape(bsz, H_B, 2, HD_B, seqlen).transpose(0, 4, 1, 2, 3)
        att = _attn_prompt(qb.reshape(shape3), kt_bf, v_bf, lam_vec, g_sub, lam_init, ATTN_HEAD_GROUP)
    else:
        (qb,) = _norm_proj(x, g_q, [w_q], [(q_scale, (BF16,))], tm)
        k_new, k_bf, v_new, v_bf = _norm_proj(x, row(p["g_kv"]), [w_k.astype(BF16), w_v],
                                              [(1.0, (F32, BF16)), (1.0, (F32, BF16))], tm)
        k_out = k_new.reshape(bsz, seqlen, H_B, 2, HD_B)
        ck, cv = cache
        past = ck.shape[1]
        ckt = ck.transpose(0, 2, 3, 4, 1).reshape(bsz, kb_w, past)
        att = _attn_sample(qb.reshape(shape3), ckt, cv, k_bf.reshape(shape3), v_bf.reshape(shape3),
                           lam_vec, g_sub, lam_init, min(SAMPLE_KEY_TILE, past))
    x = _mix_mlp(att.reshape(rows, kb_w), x, p["w_o_b"].astype(BF16), row(p["g_mix_post"][1]),
                 row(p["g_ffn_pre"][1]), p["w_up"][1].astype(BF16), p["w_down"][1].astype(BF16),
                 row(p["g_ffn_post"][1]), min(MLP_ROW_TILE, rows), FF_TILE)

    y = x.reshape(bsz, seqlen, d)
    v_out = v_new.reshape(bsz, seqlen, H_B, 2 * HD_B)
    return y, (c1[None], n1[None], m1[None, :, :, 0]), k_out, v_out


def kernel(x_prompt, x_sample, cache_k, cache_v, state_c, state_n, state_m, w_in_a, b_i_a, b_f_a, g_h_a, w_out_a,
           g_kv, w_kv, w_q_b, lambda_q1, lambda_k1, lambda_q2, lambda_k2, g_sub_b, w_o_b, g_mix_pre, g_mix_post,
           g_ffn_pre, g_ffn_post, w_up, w_down):
    assert w_in_a.shape[0] == 1 and w_q_b.shape[0] == 1 and g_mix_pre.shape[0] == 2
    dk, dv = state_c.shape[3], state_c.shape[4]
    p = {"dk": dk, "dv": dv,
         "w_in_a": w_in_a[0], "b_i_a": b_i_a[0], "b_f_a": b_f_a[0], "g_h_a": g_h_a[0], "w_out_a": w_out_a[0],
         "g_kv": g_kv, "w_kv": w_kv, "w_q_b": w_q_b[0],
         "lambda_q1": lambda_q1[0], "lambda_k1": lambda_k1[0], "lambda_q2": lambda_q2[0], "lambda_k2": lambda_k2[0],
         "g_sub_b": g_sub_b[0], "w_o_b": w_o_b[0], "g_mix_pre": g_mix_pre, "g_mix_post": g_mix_post,
         "g_ffn_pre": g_ffn_pre, "g_ffn_post": g_ffn_post, "w_up": w_up, "w_down": w_down}
    bsz = x_prompt.shape[0]
    zero_state = (jnp.zeros((bsz, H_A, dk, dv), F32), jnp.zeros((bsz, H_A, dk), F32), jnp.zeros((bsz, H_A), F32))
    y_p, (c_p, n_p, m_p), k_p, v_p = _trunk(x_prompt, zero_state, None, p, MLSTM_BLOCK)
    y_s, (c_s, n_s, m_s), k_s, v_s = _trunk(
        x_sample, (state_c[0].astype(F32), state_n[0].astype(F32), state_m[0].astype(F32)),
        (cache_k, cache_v), p, x_sample.shape[1])
    return (y_p, y_s, c_p, n_p, m_p, k_p, v_p, c_s, n_s, m_s, k_s, v_s)
```

```python
import functools
import math

import jax
import jax.numpy as jnp
from jax import lax
from jax.experimental import pallas as pl
from jax.experimental.pallas import tpu as pltpu

F32 = jnp.float32
BF16 = jnp.bfloat16

EPS = 1e-6
NEG_INF = -1e30
CHUNK = 64
LANES = 128
H_A = 8
H_B = 8
HD_B = 64
MLSTM_BLOCK = 128
MLSTM_HEAD_GROUP = 8
MLSTM_STREAMS = 1
ATTN_TILE = 512
KEY_LOOP_UNROLL = 2
ATTN_HEAD_GROUP = 2
ROW_TILE = 512
DENSE_ROW_TILE = 1024
FF_TILE = 1024
SAMPLE_KEY_TILE = 1024
VMEM_LIMIT_BYTES = 48 * 1024 * 1024
DENSE_VMEM_LIMIT_BYTES = 58 * 1024 * 1024


def _params(*sem, vmem_limit_bytes=VMEM_LIMIT_BYTES):
    return pltpu.CompilerParams(dimension_semantics=sem, vmem_limit_bytes=vmem_limit_bytes)


def _rms(x, g):
    return x * lax.rsqrt(jnp.mean(x * x, axis=-1, keepdims=True) + EPS) * g


def _log_sigmoid(x):
    return jnp.minimum(x, 0.0) - jnp.log1p(jnp.exp(-jnp.abs(x)))


def _dot_nt(a, b):
    return lax.dot_general(a, b, (((1,), (1,)), ((), ())), preferred_element_type=F32)


def _dot_tn(a, b):
    return lax.dot_general(a, b, (((0,), (0,)), ((), ())), preferred_element_type=F32)


def _norm_proj_kernel(x_ref, g_ref, *refs, specs, parts):
    n_w = len(specs)
    w_refs, o_refs = refs[:n_w], refs[n_w:]
    rows = x_ref.shape[0] // parts
    rs = [slice(r * rows, (r + 1) * rows) for r in range(parts)]
    xn = [_rms(x_ref[s, :], g_ref[...]).astype(BF16) for s in rs]
    oi = 0
    for w_ref, (scale, dtypes) in zip(w_refs, specs):
        for r in range(parts):
            acc = jnp.dot(xn[r], w_ref[...], preferred_element_type=F32)
            if scale != 1.0:
                acc = acc * scale
            for k, dt in enumerate(dtypes):
                o_refs[oi + k][rs[r], :] = acc.astype(dt)
        oi += len(dtypes)


def _norm_proj(x, g, ws, specs, tm):
    m, d = x.shape
    assert m % tm == 0 and (tm <= ROW_TILE or tm % ROW_TILE == 0)
    parts = max(1, tm // ROW_TILE)
    out_shapes, out_specs = [], []
    for w, (_, dtypes) in zip(ws, specs):
        for dt in dtypes:
            out_shapes.append(jax.ShapeDtypeStruct((m, w.shape[1]), dt))
            out_specs.append(pl.BlockSpec((tm, w.shape[1]), lambda i: (i, 0)))
    in_specs = [pl.BlockSpec((tm, d), lambda i: (i, 0)), pl.BlockSpec((1, d), lambda i: (0, 0))]
    in_specs += [pl.BlockSpec(w.shape, lambda i: (0, 0)) for w in ws]
    return pl.pallas_call(
        functools.partial(_norm_proj_kernel, specs=tuple(specs), parts=parts),
        grid=(m // tm,),
        in_specs=in_specs,
        out_specs=out_specs,
        out_shape=out_shapes,
        compiler_params=_params("parallel"),
        name="norm_proj",
    )(x, g, *ws)


def _mix_mlp_kernel(h_ref, x_ref, wo_ref, gmix_ref, gpre_ref, wup_ref, wdn_ref, gpost_ref, o_ref, *, tf, parts):
    rows = h_ref.shape[0] // parts
    rs = [slice(r * rows, (r + 1) * rows) for r in range(parts)]
    x1 = [x_ref[s, :] + _rms(jnp.dot(h_ref[s, :], wo_ref[...], preferred_element_type=F32), gmix_ref[...])
          for s in rs]
    xn = [_rms(x, gpre_ref[...]).astype(BF16) for x in x1]
    acc = [None] * parts
    for c in range(wup_ref.shape[1] // tf):
        for r in range(parts):
            up = jnp.maximum(jnp.dot(xn[r], wup_ref[:, c * tf:(c + 1) * tf], preferred_element_type=F32), 0.0)
            part = jnp.dot((up * up).astype(BF16), wdn_ref[c * tf:(c + 1) * tf, :], preferred_element_type=F32)
            acc[r] = part if acc[r] is None else acc[r] + part
    for r in range(parts):
        o_ref[rs[r], :] = x1[r] + _rms(acc[r], gpost_ref[...])


def _mix_mlp(h, x, w_o, g_mix, g_pre, w_up, w_down, g_post, tm, tf):
    m, d = x.shape
    k = h.shape[1]
    ff = w_up.shape[1]
    assert m % tm == 0 and ff % tf == 0 and (tm <= ROW_TILE or tm % ROW_TILE == 0)
    resident = lambda shape: pl.BlockSpec(shape, lambda i: (0, 0), pipeline_mode=pl.Buffered(1))
    return pl.pallas_call(
        functools.partial(_mix_mlp_kernel, tf=tf, parts=max(1, tm // ROW_TILE)),
        grid=(m // tm,),
        in_specs=[pl.BlockSpec((tm, k), lambda i: (i, 0)),
                  pl.BlockSpec((tm, d), lambda i: (i, 0)),
                  resident((k, d)),
                  resident((1, d)),
                  resident((1, d)),
                  resident((d, ff)),
                  resident((ff, d)),
                  resident((1, d))],
        out_specs=pl.BlockSpec((tm, d), lambda i: (i, 0)),
        out_shape=jax.ShapeDtypeStruct((m, d), F32),
        compiler_params=_params("parallel", vmem_limit_bytes=DENSE_VMEM_LIMIT_BYTES),
        name="mix_mlp",
    )(h, x, w_o, g_mix, g_pre, w_up, w_down, g_post)


def _mlstm_kernel(q_ref, k_ref, v_ref, o_ref, gate_ref, bcol_ref, gh_ref, c0_ref, n0_ref, m0_ref,
                  hg_ref, c_ref, n_ref, m_ref, *, nb, blk, heads, dk, dv):
    t = pl.program_id(1)

    @pl.when(t == 0)
    def _():
        c_ref[...] = c0_ref[...]
        n_ref[...] = n0_ref[...]
        m_ref[...] = m0_ref[...]

    row = lax.broadcasted_iota(jnp.int32, (blk, blk), 0)
    col = lax.broadcasted_iota(jnp.int32, (blk, blk), 1)
    causal = row >= col
    triu = (row <= col).astype(F32)

    def square(a):
        if a.shape[1] < LANES:
            a = jnp.concatenate([a, jnp.zeros((a.shape[0], LANES - a.shape[1]), F32)], axis=1)
        if a.shape[0] < LANES:
            a = jnp.concatenate([a, jnp.zeros((LANES - a.shape[0], a.shape[1]), F32)], axis=0)
        return a

    all_pairs = [(b, h) for b in range(nb) for h in range(heads)]
    for g0 in range(0, len(all_pairs), MLSTM_HEAD_GROUP):
        _mlstm_group(all_pairs[g0:g0 + MLSTM_HEAD_GROUP], q_ref, k_ref, v_ref, o_ref, gate_ref, bcol_ref, gh_ref,
                     hg_ref, c_ref, n_ref, m_ref, causal, triu, square, blk=blk, heads=heads, dk=dk, dv=dv)


def _mlstm_group(pairs, q_ref, k_ref, v_ref, o_ref, gate_ref, bcol_ref, gh_ref, hg_ref, c_ref, n_ref, m_ref,
                 causal, triu, square, *, blk, heads, dk, dv):
    hr = range(len(pairs))
    streams = sorted({b for b, _ in pairs})
    qs = [q_ref[b, :, h * dk:(h + 1) * dk] for b, h in pairs]
    ks = [k_ref[b, :, h * dk:(h + 1) * dk] for b, h in pairs]
    vs = [v_ref[b, :, h * dv:(h + 1) * dv] for b, h in pairs]
    c0s = [c_ref[b, h] for b, h in pairs]
    n0s = [n_ref[b, h] for b, h in pairs]
    m0s = [m_ref[b, h:h + 1, :] for b, h in pairs]
    ones = jnp.ones((blk, LANES), BF16)
    v1s = [jnp.concatenate([vs[h], ones], axis=1) for h in hr]
    cns = [jnp.concatenate([c0s[h], n0s[h]], axis=1).astype(BF16) for h in hr]

    qk = [_dot_nt(qs[h], ks[h]) for h in hr]
    qcn = [jnp.dot(qs[h], cns[h], preferred_element_type=F32) for h in hr]

    stats_cols, r_rows = {}, {}
    for b in streams:
        gates_row = square(gate_ref[b]).T[:, :blk]
        z_row = gates_row[0:2 * heads, :] + bcol_ref[...]
        li_row = z_row[0:heads, :]
        lf_row = _log_sigmoid(z_row[heads:2 * heads, :])
        fcum_row = jnp.dot(lf_row, triu, preferred_element_type=F32, precision=lax.Precision.HIGHEST)
        r_rows[b] = li_row - fcum_row
        stats_cols[b] = square(jnp.concatenate([li_row, fcum_row], axis=0)).T[:blk, :]
    r_row = [r_rows[b][h:h + 1, :] for b, h in pairs]
    fcs = [jnp.broadcast_to(stats_cols[b][:, heads + h:heads + h + 1], (blk, LANES)) for b, h in pairs]
    lis = [jnp.broadcast_to(stats_cols[b][:, h:h + 1], (blk, LANES)) for b, h in pairs]

    def wide(x):
        return x[:, :blk] if blk <= LANES else jnp.concatenate([x] * (blk // LANES), axis=1)

    dmat = [jnp.where(causal, wide(fcs[h]) + r_row[h], NEG_INF) for h in hr]
    a = [fcs[h] + m0s[h] for h in hr]
    m_new = [jnp.maximum(a[h], jnp.max(dmat[h], axis=1, keepdims=True)) for h in hr]
    s = [qk[h] * jnp.exp(dmat[h] - wide(m_new[h])) for h in hr]
    inter = [jnp.exp(a[h] - m_new[h]) for h in hr]
    sv1 = [jnp.dot(s[h].astype(BF16), v1s[h], preferred_element_type=F32) for h in hr]
    qn = [sv1[h][:, dv:] + inter[h] * qcn[h][:, dv:] for h in hr]
    hh = [(sv1[h][:, :dv] + inter[h] * qcn[h][:, :dv]) / jnp.maximum(jnp.abs(qn[h]), jnp.exp(-m_new[h]))
          for h in hr]

    for i, (b, h) in enumerate(pairs):
        hn = _rms(hh[i], gh_ref[:, h * dv:(h + 1) * dv])
        og = o_ref[b, :, h * dv:(h + 1) * dv].astype(F32)
        hg_ref[b, :, h * dv:(h + 1) * dv] = (hn * jax.nn.sigmoid(og)).astype(BF16)

    m_last = [m_new[h][blk - 1:blk, :] for h in hr]
    w_last = [jnp.exp(fcs[h][blk - 1:blk, :] - fcs[h] + lis[h] - m_last[h]) for h in hr]
    decay = [jnp.exp(a[h][blk - 1:blk, :] - m_last[h]) for h in hr]
    kw = [ks[h].astype(F32) * w_last[h][:, :dk] for h in hr]
    kv1 = [_dot_tn(kw[h].astype(BF16), v1s[h]) for h in hr]
    for i, (b, h) in enumerate(pairs):
        c_ref[b, h] = decay[i] * c0s[i] + kv1[i][:, :dv]
        n_ref[b, h] = decay[i] * n0s[i] + kv1[i][:, dv:]
        m_ref[b, h:h + 1, :] = m_last[i]


def _mlstm(q, k, v, o, gates, b_col, g_h, c0, n0, m0, blk, nb):
    bsz, seqlen, _ = q.shape
    heads, dk, dv = c0.shape[1], c0.shape[2], c0.shape[3]
    assert seqlen % blk == 0 and bsz % nb == 0 and dv == LANES
    row_map = lambda b, t: (b, t, 0)
    const2 = lambda b, t: (0, 0)
    st4 = lambda b, t: (b, 0, 0, 0)
    st3 = lambda b, t: (b, 0, 0)
    return pl.pallas_call(
        functools.partial(_mlstm_kernel, nb=nb, blk=blk, heads=heads, dk=dk, dv=dv),
        grid=(bsz // nb, seqlen // blk),
        in_specs=[pl.BlockSpec((nb, blk, heads * dk), row_map),
                  pl.BlockSpec((nb, blk, heads * dk), row_map),
                  pl.BlockSpec((nb, blk, heads * dv), row_map),
                  pl.BlockSpec((nb, blk, heads * dv), row_map),
                  pl.BlockSpec((nb, blk, LANES), row_map),
                  pl.BlockSpec((2 * heads, 1), const2),
                  pl.BlockSpec((1, heads * dv), const2),
                  pl.BlockSpec((nb, heads, dk, dv), st4),
                  pl.BlockSpec((nb, heads, dk, LANES), st4),
                  pl.BlockSpec((nb, heads, LANES), st3)],
        out_specs=[pl.BlockSpec((nb, blk, heads * dv), row_map),
                   pl.BlockSpec((nb, heads, dk, dv), st4),
                   pl.BlockSpec((nb, heads, dk, LANES), st4),
                   pl.BlockSpec((nb, heads, LANES), st3)],
        out_shape=[jax.ShapeDtypeStruct((bsz, seqlen, heads * dv), BF16),
                   jax.ShapeDtypeStruct((bsz, heads, dk, dv), F32),
                   jax.ShapeDtypeStruct((bsz, heads, dk, LANES), F32),
                   jax.ShapeDtypeStruct((bsz, heads, LANES), F32)],
        compiler_params=_params("parallel", "arbitrary"),
        name="mlstm",
    )(q, k, v, o, gates, b_col, g_h, c0, n0, m0)


def _stack_maps(q):
    lane = lax.broadcasted_iota(jnp.int32, q.shape, 1)
    zero = jnp.zeros_like(q)
    return jnp.concatenate([jnp.where(lane < HD_B, q, zero), jnp.where(lane >= HD_B, q, zero)], axis=0)


def _lambda(lam_ref, lam_init):
    lv = lam_ref[...]
    a = jnp.sum(lv[0:1] * lv[1:2], axis=1, keepdims=True)
    b = jnp.sum(lv[2:3] * lv[3:4], axis=1, keepdims=True)
    return jnp.exp(a) - jnp.exp(b) + lam_init


def _finish_head(acc, l, lam, g_sub, lam_init, tq):
    w = acc / l
    o = w[0:tq] - lam * w[tq:2 * tq]
    return _rms(o, g_sub) * (1.0 - lam_init)


def _attn_prompt_kernel(q_ref, kt_ref, v_ref, lam_ref, gsub_ref, o_ref, qs_sc, m_sc, l_sc, acc_sc,
                        *, tile, hg, lam_init):
    i = pl.program_id(2)
    heads = range(hg)
    for g in heads:
        qs_sc[g] = _stack_maps(q_ref[0, :, g * LANES:(g + 1) * LANES])
    m_sc[...] = jnp.full(m_sc.shape, NEG_INF, F32)
    l_sc[...] = jnp.zeros_like(l_sc)
    acc_sc[...] = jnp.zeros_like(acc_sc)
    n_lane_tiles = tile // LANES

    def step(j, masked):
        ss = [jnp.dot(qs_sc[g], kt_ref[0, j, g * LANES:(g + 1) * LANES, :], preferred_element_type=F32)
              for g in heads]
        if masked:
            row = lax.broadcasted_iota(jnp.int32, ss[0].shape, 0) & (tile - 1)
            col = lax.broadcasted_iota(jnp.int32, ss[0].shape, 1)
            ss = [jnp.where((col // CHUNK) <= (row // CHUNK), s, NEG_INF) for s in ss]
        start = pl.multiple_of(j * tile, tile)
        m_prev, m_new, alpha, ps, pv = [], [], [], [], []
        for g in heads:
            m_prev.append(m_sc[g])
            m_new.append(jnp.maximum(m_prev[g], jnp.max(ss[g], axis=1, keepdims=True)))
            alpha.append(jnp.exp2(m_prev[g] - m_new[g]))
            ps.append(jnp.exp2((ss[g] - jnp.concatenate([m_new[g]] * n_lane_tiles, axis=1)).astype(BF16)))
            pv.append(jnp.dot(ps[g], v_ref[0, g, pl.ds(start, tile), :], preferred_element_type=F32))
        for g in heads:
            p_sum = ps[g][:, 0:LANES]
            for c in range(1, n_lane_tiles):
                p_sum = p_sum + ps[g][:, c * LANES:(c + 1) * LANES]
            l_sc[g] = alpha[g] * l_sc[g] + p_sum.astype(F32)
            acc_sc[g] = alpha[g] * acc_sc[g] + pv[g]
            m_sc[g] = m_new[g]

    def body(jj, carry):
        for u in range(KEY_LOOP_UNROLL):
            step(KEY_LOOP_UNROLL * jj + u, False)
        return carry

    n_full = i // KEY_LOOP_UNROLL
    lax.fori_loop(0, n_full, body, 0)
    for rem in range(1, KEY_LOOP_UNROLL):
        @pl.when(i - KEY_LOOP_UNROLL * n_full >= rem)
        def _():
            step(KEY_LOOP_UNROLL * n_full + rem - 1, False)
    step(i, True)
    lam = _lambda(lam_ref, lam_init)
    for g in heads:
        l = jnp.sum(l_sc[g], axis=1, keepdims=True)
        o = _finish_head(acc_sc[g], l, lam, gsub_ref[:, g * LANES:(g + 1) * LANES], lam_init, tile)
        o_ref[0, :, g * LANES:(g + 1) * LANES] = o.astype(BF16)


def _attn_prompt(q, kt, v, lam_vec, g_sub, lam_init, hg):
    bsz, seqlen, width = q.shape
    tile = kt.shape[3]
    heads = width // LANES
    gw = hg * LANES
    assert seqlen % tile == 0 and tile % CHUNK == 0 and (tile & (tile - 1)) == 0 and tile % LANES == 0
    assert heads % hg == 0
    return pl.pallas_call(
        functools.partial(_attn_prompt_kernel, tile=tile, hg=hg, lam_init=lam_init),
        grid=(bsz, heads // hg, seqlen // tile),
        in_specs=[pl.BlockSpec((1, tile, gw), lambda b, h, i: (b, i, h)),
                  pl.BlockSpec((1, seqlen // tile, gw, tile), lambda b, h, i: (b, 0, h, 0)),
                  pl.BlockSpec((1, hg, seqlen, LANES), lambda b, h, i: (b, h, 0, 0)),
                  pl.BlockSpec((4, HD_B), lambda b, h, i: (0, 0)),
                  pl.BlockSpec((1, gw), lambda b, h, i: (0, h))],
        out_specs=pl.BlockSpec((1, tile, gw), lambda b, h, i: (b, i, h)),
        out_shape=jax.ShapeDtypeStruct((bsz, seqlen, width), BF16),
        scratch_shapes=[pltpu.VMEM((hg, 2 * tile, LANES), BF16),
                        pltpu.VMEM((hg, 2 * tile, LANES), F32),
                        pltpu.VMEM((hg, 2 * tile, LANES), F32),
                        pltpu.VMEM((hg, 2 * tile, LANES), F32)],
        compiler_params=_params("parallel", "parallel", "arbitrary"),
        name="attn_prompt",
    )(q, kt, v, lam_vec, g_sub)


def _kv_proj_t_kernel(x_ref, g_ref, gq_ref, wkt_ref, wv_ref, wq_ref, kt_ref, ktb_ref, v_ref, vb_ref, q_ref,
                      *, heads, q_scale, parts):
    rows = x_ref.shape[0] // parts
    rs = [slice(r * rows, (r + 1) * rows) for r in range(parts)]
    xs = [x_ref[s, :] for s in rs]
    xr = [x * lax.rsqrt(jnp.mean(x * x, axis=-1, keepdims=True) + EPS) for x in xs]
    xq = [(x * gq_ref[...]).astype(BF16) for x in xr]
    xn = [(x * g_ref[...]).astype(BF16) for x in xr]
    for r in range(parts):
        q = jnp.dot(xq[r], wq_ref[...], preferred_element_type=F32)
        q_ref[rs[r], :] = (q * q_scale).astype(BF16)
    for r in range(parts):
        kt = _dot_nt(wkt_ref[...], xn[r])
        kt_ref[0, :, rs[r]] = kt
        ktb_ref[0, r] = kt.astype(BF16)
    for r in range(parts):
        v = jnp.dot(xn[r], wv_ref[...], preferred_element_type=F32)
        v_ref[rs[r], :] = v
        for h in range(heads):
            vb_ref[0, h, rs[r], :] = v[:, h * LANES:(h + 1) * LANES].astype(BF16)


def _kv_proj_t(x, g_kv, g_q, wkt, wv, wq, q_scale, bsz, tm, kt_tile):
    rows, d = x.shape
    seqlen = rows // bsz
    width = wv.shape[1]
    heads = width // LANES
    assert seqlen % tm == 0 and tm % kt_tile == 0
    nt = seqlen // tm
    parts = tm // kt_tile
    row_blk = lambda b, i: (b * nt + i, 0)
    resident = lambda shape: pl.BlockSpec(shape, lambda b, i: (0, 0), pipeline_mode=pl.Buffered(1))
    return pl.pallas_call(
        functools.partial(_kv_proj_t_kernel, heads=heads, q_scale=q_scale, parts=parts),
        grid=(bsz, nt),
        in_specs=[pl.BlockSpec((tm, d), row_blk),
                  resident((1, d)),
                  resident((1, d)),
                  resident((width, d)),
                  resident((d, width)),
                  resident((d, width))],
        out_specs=[pl.BlockSpec((1, width, tm), lambda b, i: (b, 0, i)),
                   pl.BlockSpec((1, parts, width, kt_tile), lambda b, i: (b, i, 0, 0)),
                   pl.BlockSpec((tm, width), row_blk),
                   pl.BlockSpec((1, heads, tm, LANES), lambda b, i: (b, 0, i, 0)),
                   pl.BlockSpec((tm, width), row_blk)],
        out_shape=[jax.ShapeDtypeStruct((bsz, width, seqlen), F32),
                   jax.ShapeDtypeStruct((bsz, seqlen // kt_tile, width, kt_tile), BF16),
                   jax.ShapeDtypeStruct((rows, width), F32),
                   jax.ShapeDtypeStruct((bsz, heads, seqlen, LANES), BF16),
                   jax.ShapeDtypeStruct((rows, width), BF16)],
        compiler_params=_params("parallel", "parallel", vmem_limit_bytes=DENSE_VMEM_LIMIT_BYTES),
        name="kv_proj_t",
    )(x, g_kv, g_q, wkt, wv, wq)


def _attn_sample_kernel(q_ref, ckt_ref, cv_ref, kn_ref, vn_ref, lam_ref, gsub_ref, o_ref, qs_sc, m_sc, l_sc, acc_sc,
                        *, lq, heads, past, lam_init):
    j = pl.program_id(1)

    @pl.when(j == 0)
    def _():
        for h in range(heads):
            qs_sc[h] = _stack_maps(q_ref[0, :, h * LANES:(h + 1) * LANES])
        m_sc[...] = jnp.full(m_sc.shape, NEG_INF, F32)
        l_sc[...] = jnp.zeros_like(l_sc)
        acc_sc[...] = jnp.zeros_like(acc_sc)

    hr = range(heads)

    def update(ss, vts):
        m_prev = [m_sc[h] for h in hr]
        m_new = [jnp.maximum(m_prev[h], jnp.max(ss[h], axis=1, keepdims=True)) for h in hr]
        alpha = [jnp.exp2(m_prev[h] - m_new[h]) for h in hr]
        ps = [jnp.exp2(ss[h] - m_new[h]) for h in hr]
        pv = [jnp.dot(ps[h].astype(BF16), vts[h], preferred_element_type=F32) for h in hr]
        for h in hr:
            l_sc[h] = alpha[h] * l_sc[h] + jnp.sum(ps[h], axis=1, keepdims=True)
            acc_sc[h] = alpha[h] * acc_sc[h] + pv[h]
            m_sc[h] = m_new[h]

    v_heads = pltpu.einshape("phd->hpd", cv_ref[0])
    update([jnp.dot(qs_sc[h], ckt_ref[0, h * LANES:(h + 1) * LANES, :].astype(BF16), preferred_element_type=F32)
            for h in hr],
           [v_heads[h].astype(BF16) for h in hr])

    @pl.when(j == pl.num_programs(1) - 1)
    def _():
        row = lax.broadcasted_iota(jnp.int32, (2 * lq, lq), 0)
        row = jnp.where(row >= lq, row - lq, row)
        col = lax.broadcasted_iota(jnp.int32, (2 * lq, lq), 1)
        mask = ((past + col) // CHUNK) <= ((past + row) // CHUNK)
        lam = _lambda(lam_ref, lam_init)
        hs = [slice(h * LANES, (h + 1) * LANES) for h in hr]
        update([jnp.where(mask, _dot_nt(qs_sc[h], kn_ref[0, :, hs[h]]), NEG_INF) for h in hr],
               [vn_ref[0, :, hs[h]] for h in hr])
        for h in hr:
            o = _finish_head(acc_sc[h], l_sc[h], lam, gsub_ref[:, hs[h]], lam_init, lq)
            o_ref[0, :, hs[h]] = o.astype(BF16)


def _attn_sample(q, cache_kt, cache_v, k_new, v_new, lam_vec, g_sub, lam_init, tkc):
    bsz, lq, width = q.shape
    past = cache_kt.shape[2]
    heads = width // LANES
    assert past % tkc == 0 and past % CHUNK == 0
    new_blk = pl.BlockSpec((1, lq, width), lambda b, j: (b, 0, 0))
    return pl.pallas_call(
        functools.partial(_attn_sample_kernel, lq=lq, heads=heads, past=past, lam_init=lam_init),
        grid=(bsz, past // tkc),
        in_specs=[new_blk,
                  pl.BlockSpec((1, width, tkc), lambda b, j: (b, 0, j)),
                  pl.BlockSpec((1, tkc, heads, LANES), lambda b, j: (b, j, 0, 0)),
                  new_blk,
                  new_blk,
                  pl.BlockSpec((4, HD_B), lambda b, j: (0, 0)),
                  pl.BlockSpec((1, width), lambda b, j: (0, 0))],
        out_specs=new_blk,
        out_shape=jax.ShapeDtypeStruct((bsz, lq, width), BF16),
        scratch_shapes=[pltpu.VMEM((heads, 2 * lq, LANES), BF16),
                        pltpu.VMEM((heads, 2 * lq, 1), F32),
                        pltpu.VMEM((heads, 2 * lq, 1), F32),
                        pltpu.VMEM((heads, 2 * lq, LANES), F32)],
        compiler_params=_params("parallel", "arbitrary"),
        name="attn_sample",
    )(q, cache_kt, cache_v, k_new, v_new, lam_vec, g_sub)


def _lambda_init(layer_idx):
    return 0.8 - 0.6 * math.exp(-0.3 * layer_idx)


def _trunk(x3, state, cache, p, mlstm_blk):
    bsz, seqlen, d = x3.shape
    rows = bsz * seqlen
    tm = min(ROW_TILE, rows)
    x = x3.reshape(rows, d)
    row = lambda g: g.reshape(1, -1)

    qk_w = H_A * p["dk"]
    v_w = H_A * p["dv"]
    w_in = p["w_in_a"]
    w_parts = [w_in[:, 0:qk_w], w_in[:, qk_w:2 * qk_w], w_in[:, 2 * qk_w:2 * qk_w + v_w],
               w_in[:, 2 * qk_w + v_w:2 * qk_w + 2 * v_w]]
    w_gate = jnp.pad(w_in[:, 2 * qk_w + 2 * v_w:], ((0, 0), (0, LANES - 2 * H_A)))
    ws = [w.astype(BF16) for w in w_parts] + [w_gate.astype(BF16)]
    specs = [(1.0, (BF16,)), (p["dk"] ** -0.5, (BF16,)), (1.0, (BF16,)), (1.0, (BF16,)), (1.0, (F32,))]
    q, k, v, o, gates = _norm_proj(x, row(p["g_mix_pre"][0]), ws, specs, min(DENSE_ROW_TILE, rows))

    bias = jnp.concatenate([p["b_i_a"], p["b_f_a"]]).astype(F32)
    b_col = bias.reshape(2 * H_A, 1)
    c0, n0, m0 = state
    m0b = jnp.broadcast_to(m0[:, :, None], (bsz, H_A, LANES))
    n0b = jnp.broadcast_to(n0[:, :, :, None], n0.shape + (LANES,))
    r3 = lambda a: a.reshape(bsz, seqlen, a.shape[-1])
    hg, c1, n1b, m1 = _mlstm(r3(q), r3(k), r3(v), r3(o), r3(gates), b_col, row(p["g_h_a"]), c0, n0b, m0b,
                             mlstm_blk, MLSTM_STREAMS)
    n1 = n1b[:, :, :, 0]
    x = _mix_mlp(hg.reshape(rows, v_w), x, p["w_out_a"].astype(BF16), row(p["g_mix_post"][0]),
                 row(p["g_ffn_pre"][0]), p["w_up"][0].astype(BF16), p["w_down"][0].astype(BF16),
                 row(p["g_ffn_post"][0]), min(DENSE_ROW_TILE, rows), FF_TILE)

    kb_w = H_B * 2 * HD_B
    w_k = p["w_kv"][:, :kb_w]
    w_v = p["w_kv"][:, kb_w:].astype(BF16)
    q_scale = HD_B ** -0.5 * math.log2(math.e)
    w_q = p["w_q_b"].astype(BF16)
    g_q = row(p["g_mix_pre"][1])

    lam_init = _lambda_init(1)
    lam_vec = jnp.stack([p["lambda_q1"], p["lambda_k1"], p["lambda_q2"], p["lambda_k2"]]).astype(F32)
    g_sub = row(p["g_sub_b"])
    shape3 = (bsz, seqlen, kb_w)
    if cache is None:
        kt_new, kt_bf, v_new, v_bf, qb = _kv_proj_t(x, row(p["g_kv"]), g_q, w_k.T.astype(BF16), w_v, w_q, q_scale,
                                                    bsz, min(DENSE_ROW_TILE, seqlen), ATTN_TILE)
        k_out = kt_new.reshape(bsz, H_B, 2, HD_B, seqlen).transpose(0, 4, 1, 2, 3)
        att = _attn_prompt(qb.reshape(shape3), kt_bf, v_bf, lam_vec, g_sub, lam_init, ATTN_HEAD_GROUP)
    else:
        (qb,) = _norm_proj(x, g_q, [w_q], [(q_scale, (BF16,))], tm)
        k_new, k_bf, v_new, v_bf = _norm_proj(x, row(p["g_kv"]), [w_k.astype(BF16), w_v],
                                              [(1.0, (F32, BF16)), (1.0, (F32, BF16))], tm)
        k_out = k_new.reshape(bsz, seqlen, H_B, 2, HD_B)
        ck, cv = cache
        past = ck.shape[1]
        ckt = ck.transpose(0, 2, 3, 4, 1).reshape(bsz, kb_w, past)
        att = _attn_sample(qb.reshape(shape3), ckt, cv, k_bf.reshape(shape3), v_bf.reshape(shape3),
                           lam_vec, g_sub, lam_init, min(SAMPLE_KEY_TILE, past))
    x = _mix_mlp(att.reshape(rows, kb_w), x, p["w_o_b"].astype(BF16), row(p["g_mix_post"][1]),
                 row(p["g_ffn_pre"][1]), p["w_up"][1].astype(BF16), p["w_down"][1].astype(BF16),
                 row(p["g_ffn_post"][1]), min(DENSE_ROW_TILE, rows), FF_TILE)

    y = x.reshape(bsz, seqlen, d)
    v_out = v_new.reshape(bsz, seqlen, H_B, 2 * HD_B)
    return y, (c1[None], n1[None], m1[None, :, :, 0]), k_out, v_out


def kernel(x_prompt, x_sample, cache_k, cache_v, state_c, state_n, state_m, w_in_a, b_i_a, b_f_a, g_h_a, w_out_a,
           g_kv, w_kv, w_q_b, lambda_q1, lambda_k1, lambda_q2, lambda_k2, g_sub_b, w_o_b, g_mix_pre, g_mix_post,
           g_ffn_pre, g_ffn_post, w_up, w_down):
    assert w_in_a.shape[0] == 1 and w_q_b.shape[0] == 1 and g_mix_pre.shape[0] == 2
    dk, dv = state_c.shape[3], state_c.shape[4]
    p = {"dk": dk, "dv": dv,
         "w_in_a": w_in_a[0], "b_i_a": b_i_a[0], "b_f_a": b_f_a[0], "g_h_a": g_h_a[0], "w_out_a": w_out_a[0],
         "g_kv": g_kv, "w_kv": w_kv, "w_q_b": w_q_b[0],
         "lambda_q1": lambda_q1[0], "lambda_k1": lambda_k1[0], "lambda_q2": lambda_q2[0], "lambda_k2": lambda_k2[0],
         "g_sub_b": g_sub_b[0], "w_o_b": w_o_b[0], "g_mix_pre": g_mix_pre, "g_mix_post": g_mix_post,
         "g_ffn_pre": g_ffn_pre, "g_ffn_post": g_ffn_post, "w_up": w_up, "w_down": w_down}
    bsz = x_prompt.shape[0]
    zero_state = (jnp.zeros((bsz, H_A, dk, dv), F32), jnp.zeros((bsz, H_A, dk), F32), jnp.zeros((bsz, H_A), F32))
    y_p, (c_p, n_p, m_p), k_p, v_p = _trunk(x_prompt, zero_state, None, p, MLSTM_BLOCK)
    y_s, (c_s, n_s, m_s), k_s, v_s = _trunk(
        x_sample, (state_c[0].astype(F32), state_n[0].astype(F32), state_m[0].astype(F32)),
        (cache_k, cache_v), p, x_sample.shape[1])
    return (y_p, y_s, c_p, n_p, m_p, k_p, v_p, c_s, n_s, m_s, k_s, v_s)
```

```python
import functools
import math

import jax
import jax.numpy as jnp
from jax import lax
from jax.experimental import pallas as pl
from jax.experimental.pallas import tpu as pltpu

F32 = jnp.float32
BF16 = jnp.bfloat16

EPS = 1e-6
NEG_INF = -1e30
CHUNK = 64
LANES = 128
H_A = 8
H_B = 8
HD_B = 64
MLSTM_BLOCK = 128
MLSTM_HEAD_GROUP = 8
MLSTM_STREAMS = 1
MLSTM_BLOCKS_PER_STEP = 4
ATTN_TILE = 512
KEY_LOOP_UNROLL = 2
ATTN_HEAD_GROUP = 2
ATTN_QUERY_TILES_PER_STEP = 4
ROW_TILE = 512
DENSE_ROW_TILE = 1024
FF_TILE = 1024
SAMPLE_KEY_TILE = 1024
VMEM_LIMIT_BYTES = 48 * 1024 * 1024
DENSE_VMEM_LIMIT_BYTES = 58 * 1024 * 1024


def _params(*sem, vmem_limit_bytes=VMEM_LIMIT_BYTES):
    return pltpu.CompilerParams(dimension_semantics=sem, vmem_limit_bytes=vmem_limit_bytes)


def _rms(x, g):
    return x * lax.rsqrt(jnp.mean(x * x, axis=-1, keepdims=True) + EPS) * g


def _log_sigmoid(x):
    return jnp.minimum(x, 0.0) - jnp.log1p(jnp.exp(-jnp.abs(x)))


def _dot_nt(a, b):
    return lax.dot_general(a, b, (((1,), (1,)), ((), ())), preferred_element_type=F32)


def _dot_tn(a, b):
    return lax.dot_general(a, b, (((0,), (0,)), ((), ())), preferred_element_type=F32)


def _norm_proj_kernel(x_ref, g_ref, *refs, specs, parts):
    n_w = len(specs)
    w_refs, o_refs = refs[:n_w], refs[n_w:]
    rows = x_ref.shape[0] // parts
    rs = [slice(r * rows, (r + 1) * rows) for r in range(parts)]
    xn = [_rms(x_ref[s, :], g_ref[...]).astype(BF16) for s in rs]
    oi = 0
    for w_ref, (scale, dtypes) in zip(w_refs, specs):
        for r in range(parts):
            acc = jnp.dot(xn[r], w_ref[...], preferred_element_type=F32)
            if scale != 1.0:
                acc = acc * scale
            for k, dt in enumerate(dtypes):
                o_refs[oi + k][rs[r], :] = acc.astype(dt)
        oi += len(dtypes)


def _norm_proj(x, g, ws, specs, tm):
    m, d = x.shape
    assert m % tm == 0 and (tm <= ROW_TILE or tm % ROW_TILE == 0)
    parts = max(1, tm // ROW_TILE)
    out_shapes, out_specs = [], []
    for w, (_, dtypes) in zip(ws, specs):
        for dt in dtypes:
            out_shapes.append(jax.ShapeDtypeStruct((m, w.shape[1]), dt))
            out_specs.append(pl.BlockSpec((tm, w.shape[1]), lambda i: (i, 0)))
    in_specs = [pl.BlockSpec((tm, d), lambda i: (i, 0)), pl.BlockSpec((1, d), lambda i: (0, 0))]
    in_specs += [pl.BlockSpec(w.shape, lambda i: (0, 0)) for w in ws]
    return pl.pallas_call(
        functools.partial(_norm_proj_kernel, specs=tuple(specs), parts=parts),
        grid=(m // tm,),
        in_specs=in_specs,
        out_specs=out_specs,
        out_shape=out_shapes,
        compiler_params=_params("parallel"),
        name="norm_proj",
    )(x, g, *ws)


def _mix_mlp_kernel(h_ref, x_ref, wo_ref, gmix_ref, gpre_ref, wup_ref, wdn_ref, gpost_ref, o_ref, *, tf, parts):
    rows = h_ref.shape[0] // parts
    rs = [slice(r * rows, (r + 1) * rows) for r in range(parts)]
    x1 = [x_ref[s, :] + _rms(jnp.dot(h_ref[s, :], wo_ref[...], preferred_element_type=F32), gmix_ref[...])
          for s in rs]
    xn = [_rms(x, gpre_ref[...]).astype(BF16) for x in x1]
    acc = [None] * parts
    for c in range(wup_ref.shape[1] // tf):
        for r in range(parts):
            up = jnp.maximum(jnp.dot(xn[r], wup_ref[:, c * tf:(c + 1) * tf], preferred_element_type=F32), 0.0)
            part = jnp.dot((up * up).astype(BF16), wdn_ref[c * tf:(c + 1) * tf, :], preferred_element_type=F32)
            acc[r] = part if acc[r] is None else acc[r] + part
    for r in range(parts):
        o_ref[rs[r], :] = x1[r] + _rms(acc[r], gpost_ref[...])


def _mix_mlp(h, x, w_o, g_mix, g_pre, w_up, w_down, g_post, tm, tf):
    m, d = x.shape
    k = h.shape[1]
    ff = w_up.shape[1]
    assert m % tm == 0 and ff % tf == 0 and (tm <= ROW_TILE or tm % ROW_TILE == 0)
    resident = lambda shape: pl.BlockSpec(shape, lambda i: (0, 0), pipeline_mode=pl.Buffered(1))
    return pl.pallas_call(
        functools.partial(_mix_mlp_kernel, tf=tf, parts=max(1, tm // ROW_TILE)),
        grid=(m // tm,),
        in_specs=[pl.BlockSpec((tm, k), lambda i: (i, 0)),
                  pl.BlockSpec((tm, d), lambda i: (i, 0)),
                  resident((k, d)),
                  resident((1, d)),
                  resident((1, d)),
                  resident((d, ff)),
                  resident((ff, d)),
                  resident((1, d))],
        out_specs=pl.BlockSpec((tm, d), lambda i: (i, 0)),
        out_shape=jax.ShapeDtypeStruct((m, d), F32),
        compiler_params=_params("parallel", vmem_limit_bytes=DENSE_VMEM_LIMIT_BYTES),
        name="mix_mlp",
    )(h, x, w_o, g_mix, g_pre, w_up, w_down, g_post)


def _mlstm_kernel(q_ref, k_ref, v_ref, o_ref, gate_ref, bcol_ref, gh_ref, c0_ref, n0_ref, m0_ref,
                  hg_ref, c_ref, n_ref, m_ref, *, nb, blk, sub, heads, dk, dv):
    t = pl.program_id(1)

    @pl.when(t == 0)
    def _():
        c_ref[...] = c0_ref[...]
        n_ref[...] = n0_ref[...]
        m_ref[...] = m0_ref[...]

    row = lax.broadcasted_iota(jnp.int32, (blk, blk), 0)
    col = lax.broadcasted_iota(jnp.int32, (blk, blk), 1)
    causal = row >= col
    triu = (row <= col).astype(F32)

    def square(a):
        if a.shape[1] < LANES:
            a = jnp.concatenate([a, jnp.zeros((a.shape[0], LANES - a.shape[1]), F32)], axis=1)
        if a.shape[0] < LANES:
            a = jnp.concatenate([a, jnp.zeros((LANES - a.shape[0], a.shape[1]), F32)], axis=0)
        return a

    all_pairs = [(b, h) for b in range(nb) for h in range(heads)]

    def one_block(rows):
        for g0 in range(0, len(all_pairs), MLSTM_HEAD_GROUP):
            _mlstm_group(all_pairs[g0:g0 + MLSTM_HEAD_GROUP], rows, q_ref, k_ref, v_ref, o_ref, gate_ref, bcol_ref,
                         gh_ref, hg_ref, c_ref, n_ref, m_ref, causal, triu, square,
                         blk=blk, heads=heads, dk=dk, dv=dv)

    if sub == 1:
        one_block(slice(None))
    else:
        def body(j, carry):
            one_block(pl.ds(pl.multiple_of(j * blk, blk), blk))
            return carry

        lax.fori_loop(0, sub, body, 0)


def _mlstm_group(pairs, rows, q_ref, k_ref, v_ref, o_ref, gate_ref, bcol_ref, gh_ref, hg_ref, c_ref, n_ref, m_ref,
                 causal, triu, square, *, blk, heads, dk, dv):
    hr = range(len(pairs))
    streams = sorted({b for b, _ in pairs})
    qs = [q_ref[b, rows, h * dk:(h + 1) * dk] for b, h in pairs]
    ks = [k_ref[b, rows, h * dk:(h + 1) * dk] for b, h in pairs]
    vs = [v_ref[b, rows, h * dv:(h + 1) * dv] for b, h in pairs]
    c0s = [c_ref[b, h] for b, h in pairs]
    n0s = [n_ref[b, h] for b, h in pairs]
    m0s = [m_ref[b, h:h + 1, :] for b, h in pairs]
    ones = jnp.ones((blk, LANES), BF16)
    v1s = [jnp.concatenate([vs[h], ones], axis=1) for h in hr]
    cns = [jnp.concatenate([c0s[h], n0s[h]], axis=1).astype(BF16) for h in hr]

    qk = [_dot_nt(qs[h], ks[h]) for h in hr]
    qcn = [jnp.dot(qs[h], cns[h], preferred_element_type=F32) for h in hr]

    stats_cols, r_rows = {}, {}
    for b in streams:
        gates_row = square(gate_ref[b, rows, :]).T[:, :blk]
        z_row = gates_row[0:2 * heads, :] + bcol_ref[...]
        li_row = z_row[0:heads, :]
        lf_row = _log_sigmoid(z_row[heads:2 * heads, :])
        fcum_row = jnp.dot(lf_row, triu, preferred_element_type=F32, precision=lax.Precision.HIGHEST)
        r_rows[b] = li_row - fcum_row
        stats_cols[b] = square(jnp.concatenate([li_row, fcum_row], axis=0)).T[:blk, :]
    r_row = [r_rows[b][h:h + 1, :] for b, h in pairs]
    fcs = [jnp.broadcast_to(stats_cols[b][:, heads + h:heads + h + 1], (blk, LANES)) for b, h in pairs]
    lis = [jnp.broadcast_to(stats_cols[b][:, h:h + 1], (blk, LANES)) for b, h in pairs]

    def wide(x):
        return x[:, :blk] if blk <= LANES else jnp.concatenate([x] * (blk // LANES), axis=1)

    dmat = [jnp.where(causal, wide(fcs[h]) + r_row[h], NEG_INF) for h in hr]
    a = [fcs[h] + m0s[h] for h in hr]
    m_new = [jnp.maximum(a[h], jnp.max(dmat[h], axis=1, keepdims=True)) for h in hr]
    s = [qk[h] * jnp.exp(dmat[h] - wide(m_new[h])) for h in hr]
    inter = [jnp.exp(a[h] - m_new[h]) for h in hr]
    sv1 = [jnp.dot(s[h].astype(BF16), v1s[h], preferred_element_type=F32) for h in hr]
    qn = [sv1[h][:, dv:] + inter[h] * qcn[h][:, dv:] for h in hr]
    hh = [(sv1[h][:, :dv] + inter[h] * qcn[h][:, :dv]) / jnp.maximum(jnp.abs(qn[h]), jnp.exp(-m_new[h]))
          for h in hr]

    for i, (b, h) in enumerate(pairs):
        hn = _rms(hh[i], gh_ref[:, h * dv:(h + 1) * dv])
        og = o_ref[b, rows, h * dv:(h + 1) * dv].astype(F32)
        hg_ref[b, rows, h * dv:(h + 1) * dv] = (hn * jax.nn.sigmoid(og)).astype(BF16)

    m_last = [m_new[h][blk - 1:blk, :] for h in hr]
    w_last = [jnp.exp(fcs[h][blk - 1:blk, :] - fcs[h] + lis[h] - m_last[h]) for h in hr]
    decay = [jnp.exp(a[h][blk - 1:blk, :] - m_last[h]) for h in hr]
    kw = [ks[h].astype(F32) * w_last[h][:, :dk] for h in hr]
    kv1 = [_dot_tn(kw[h].astype(BF16), v1s[h]) for h in hr]
    for i, (b, h) in enumerate(pairs):
        c_ref[b, h] = decay[i] * c0s[i] + kv1[i][:, :dv]
        n_ref[b, h] = decay[i] * n0s[i] + kv1[i][:, dv:]
        m_ref[b, h:h + 1, :] = m_last[i]


def _mlstm(q, k, v, o, gates, b_col, g_h, c0, n0, m0, blk, nb, sub):
    bsz, seqlen, _ = q.shape
    heads, dk, dv = c0.shape[1], c0.shape[2], c0.shape[3]
    step = blk * sub
    assert seqlen % step == 0 and bsz % nb == 0 and dv == LANES
    row_map = lambda b, t: (b, t, 0)
    const2 = lambda b, t: (0, 0)
    st4 = lambda b, t: (b, 0, 0, 0)
    st3 = lambda b, t: (b, 0, 0)
    return pl.pallas_call(
        functools.partial(_mlstm_kernel, nb=nb, blk=blk, sub=sub, heads=heads, dk=dk, dv=dv),
        grid=(bsz // nb, seqlen // step),
        in_specs=[pl.BlockSpec((nb, step, heads * dk), row_map),
                  pl.BlockSpec((nb, step, heads * dk), row_map),
                  pl.BlockSpec((nb, step, heads * dv), row_map),
                  pl.BlockSpec((nb, step, heads * dv), row_map),
                  pl.BlockSpec((nb, step, LANES), row_map),
                  pl.BlockSpec((2 * heads, 1), const2),
                  pl.BlockSpec((1, heads * dv), const2),
                  pl.BlockSpec((nb, heads, dk, dv), st4),
                  pl.BlockSpec((nb, heads, dk, LANES), st4),
                  pl.BlockSpec((nb, heads, LANES), st3)],
        out_specs=[pl.BlockSpec((nb, step, heads * dv), row_map),
                   pl.BlockSpec((nb, heads, dk, dv), st4),
                   pl.BlockSpec((nb, heads, dk, LANES), st4),
                   pl.BlockSpec((nb, heads, LANES), st3)],
        out_shape=[jax.ShapeDtypeStruct((bsz, seqlen, heads * dv), BF16),
                   jax.ShapeDtypeStruct((bsz, heads, dk, dv), F32),
                   jax.ShapeDtypeStruct((bsz, heads, dk, LANES), F32),
                   jax.ShapeDtypeStruct((bsz, heads, LANES), F32)],
        compiler_params=_params("parallel", "arbitrary"),
        name="mlstm",
    )(q, k, v, o, gates, b_col, g_h, c0, n0, m0)


def _stack_maps(q):
    lane = lax.broadcasted_iota(jnp.int32, q.shape, 1)
    zero = jnp.zeros_like(q)
    return jnp.concatenate([jnp.where(lane < HD_B, q, zero), jnp.where(lane >= HD_B, q, zero)], axis=0)


def _lambda(lam_ref, lam_init):
    lv = lam_ref[...]
    a = jnp.sum(lv[0:1] * lv[1:2], axis=1, keepdims=True)
    b = jnp.sum(lv[2:3] * lv[3:4], axis=1, keepdims=True)
    return jnp.exp(a) - jnp.exp(b) + lam_init


def _finish_head(acc, l, lam, g_sub, lam_init, tq):
    w = acc / l
    o = w[0:tq] - lam * w[tq:2 * tq]
    return _rms(o, g_sub) * (1.0 - lam_init)


def _attn_prompt_kernel(q_ref, kt_ref, v_ref, lam_ref, gsub_ref, o_ref, qs_sc, m_sc, l_sc, acc_sc,
                        *, tile, hg, qt, lam_init):
    first = pl.program_id(2) * qt
    lam = _lambda(lam_ref, lam_init)

    def one_query_tile(t, carry):
        _attn_query_tile(first + t, pl.ds(pl.multiple_of(t * tile, tile), tile), q_ref, kt_ref, v_ref, gsub_ref, o_ref,
                         qs_sc, m_sc, l_sc, acc_sc, lam, tile=tile, hg=hg, lam_init=lam_init)
        return carry

    lax.fori_loop(0, qt, one_query_tile, 0)


def _attn_query_tile(i, rows, q_ref, kt_ref, v_ref, gsub_ref, o_ref, qs_sc, m_sc, l_sc, acc_sc, lam,
                     *, tile, hg, lam_init):
    heads = range(hg)
    for g in heads:
        qs_sc[g] = _stack_maps(q_ref[0, rows, g * LANES:(g + 1) * LANES])
    m_sc[...] = jnp.full(m_sc.shape, NEG_INF, F32)
    l_sc[...] = jnp.zeros_like(l_sc)
    acc_sc[...] = jnp.zeros_like(acc_sc)
    n_lane_tiles = tile // LANES

    def step(j, masked):
        ss = [jnp.dot(qs_sc[g], kt_ref[0, j, g * LANES:(g + 1) * LANES, :], preferred_element_type=F32)
              for g in heads]
        if masked:
            row = lax.broadcasted_iota(jnp.int32, ss[0].shape, 0) & (tile - 1)
            col = lax.broadcasted_iota(jnp.int32, ss[0].shape, 1)
            ss = [jnp.where((col // CHUNK) <= (row // CHUNK), s, NEG_INF) for s in ss]
        start = pl.multiple_of(j * tile, tile)
        m_prev, m_new, alpha, ps, pv = [], [], [], [], []
        for g in heads:
            m_prev.append(m_sc[g])
            m_new.append(jnp.maximum(m_prev[g], jnp.max(ss[g], axis=1, keepdims=True)))
            alpha.append(jnp.exp2(m_prev[g] - m_new[g]))
            ps.append(jnp.exp2((ss[g] - jnp.concatenate([m_new[g]] * n_lane_tiles, axis=1)).astype(BF16)))
            pv.append(jnp.dot(ps[g], v_ref[0, g, pl.ds(start, tile), :], preferred_element_type=F32))
        for g in heads:
            p_sum = ps[g][:, 0:LANES]
            for c in range(1, n_lane_tiles):
                p_sum = p_sum + ps[g][:, c * LANES:(c + 1) * LANES]
            l_sc[g] = alpha[g] * l_sc[g] + p_sum.astype(F32)
            acc_sc[g] = alpha[g] * acc_sc[g] + pv[g]
            m_sc[g] = m_new[g]

    def body(jj, carry):
        for u in range(KEY_LOOP_UNROLL):
            step(KEY_LOOP_UNROLL * jj + u, False)
        return carry

    n_full = i // KEY_LOOP_UNROLL
    lax.fori_loop(0, n_full, body, 0)
    for rem in range(1, KEY_LOOP_UNROLL):
        @pl.when(i - KEY_LOOP_UNROLL * n_full >= rem)
        def _():
            step(KEY_LOOP_UNROLL * n_full + rem - 1, False)
    step(i, True)
    for g in heads:
        l = jnp.sum(l_sc[g], axis=1, keepdims=True)
        o = _finish_head(acc_sc[g], l, lam, gsub_ref[:, g * LANES:(g + 1) * LANES], lam_init, tile)
        o_ref[0, rows, g * LANES:(g + 1) * LANES] = o.astype(BF16)


def _attn_prompt(q, kt, v, lam_vec, g_sub, lam_init, hg, qt):
    bsz, seqlen, width = q.shape
    tile = kt.shape[3]
    heads = width // LANES
    gw = hg * LANES
    assert seqlen % (tile * qt) == 0 and tile % CHUNK == 0 and (tile & (tile - 1)) == 0 and tile % LANES == 0
    assert heads % hg == 0
    return pl.pallas_call(
        functools.partial(_attn_prompt_kernel, tile=tile, hg=hg, qt=qt, lam_init=lam_init),
        grid=(bsz, heads // hg, seqlen // (tile * qt)),
        in_specs=[pl.BlockSpec((1, tile * qt, gw), lambda b, h, i: (b, i, h)),
                  pl.BlockSpec((1, seqlen // tile, gw, tile), lambda b, h, i: (b, 0, h, 0)),
                  pl.BlockSpec((1, hg, seqlen, LANES), lambda b, h, i: (b, h, 0, 0)),
                  pl.BlockSpec((4, HD_B), lambda b, h, i: (0, 0)),
                  pl.BlockSpec((1, gw), lambda b, h, i: (0, h))],
        out_specs=pl.BlockSpec((1, tile * qt, gw), lambda b, h, i: (b, i, h)),
        out_shape=jax.ShapeDtypeStruct((bsz, seqlen, width), BF16),
        scratch_shapes=[pltpu.VMEM((hg, 2 * tile, LANES), BF16),
                        pltpu.VMEM((hg, 2 * tile, LANES), F32),
                        pltpu.VMEM((hg, 2 * tile, LANES), F32),
                        pltpu.VMEM((hg, 2 * tile, LANES), F32)],
        compiler_params=_params("parallel", "parallel", "arbitrary"),
        name="attn_prompt",
    )(q, kt, v, lam_vec, g_sub)


def _kv_proj_t_kernel(x_ref, g_ref, gq_ref, wkt_ref, wv_ref, wq_ref, kt_ref, ktb_ref, v_ref, vb_ref, q_ref,
                      *, heads, q_scale, parts):
    rows = x_ref.shape[0] // parts
    rs = [slice(r * rows, (r + 1) * rows) for r in range(parts)]
    xs = [x_ref[s, :] for s in rs]
    xr = [x * lax.rsqrt(jnp.mean(x * x, axis=-1, keepdims=True) + EPS) for x in xs]
    xq = [(x * gq_ref[...]).astype(BF16) for x in xr]
    xn = [(x * g_ref[...]).astype(BF16) for x in xr]
    for r in range(parts):
        q = jnp.dot(xq[r], wq_ref[...], preferred_element_type=F32)
        q_ref[rs[r], :] = (q * q_scale).astype(BF16)
    for r in range(parts):
        kt = _dot_nt(wkt_ref[...], xn[r])
        kt_ref[0, :, rs[r]] = kt
        ktb_ref[0, r] = kt.astype(BF16)
    for r in range(parts):
        v = jnp.dot(xn[r], wv_ref[...], preferred_element_type=F32)
        v_ref[rs[r], :] = v
        for h in range(heads):
            vb_ref[0, h, rs[r], :] = v[:, h * LANES:(h + 1) * LANES].astype(BF16)


def _kv_proj_t(x, g_kv, g_q, wkt, wv, wq, q_scale, bsz, tm, kt_tile):
    rows, d = x.shape
    seqlen = rows // bsz
    width = wv.shape[1]
    heads = width // LANES
    assert seqlen % tm == 0 and tm % kt_tile == 0
    nt = seqlen // tm
    parts = tm // kt_tile
    row_blk = lambda b, i: (b * nt + i, 0)
    resident = lambda shape: pl.BlockSpec(shape, lambda b, i: (0, 0), pipeline_mode=pl.Buffered(1))
    return pl.pallas_call(
        functools.partial(_kv_proj_t_kernel, heads=heads, q_scale=q_scale, parts=parts),
        grid=(bsz, nt),
        in_specs=[pl.BlockSpec((tm, d), row_blk),
                  resident((1, d)),
                  resident((1, d)),
                  resident((width, d)),
                  resident((d, width)),
                  resident((d, width))],
        out_specs=[pl.BlockSpec((1, width, tm), lambda b, i: (b, 0, i)),
                   pl.BlockSpec((1, parts, width, kt_tile), lambda b, i: (b, i, 0, 0)),
                   pl.BlockSpec((tm, width), row_blk),
                   pl.BlockSpec((1, heads, tm, LANES), lambda b, i: (b, 0, i, 0)),
                   pl.BlockSpec((tm, width), row_blk)],
        out_shape=[jax.ShapeDtypeStruct((bsz, width, seqlen), F32),
                   jax.ShapeDtypeStruct((bsz, seqlen // kt_tile, width, kt_tile), BF16),
                   jax.ShapeDtypeStruct((rows, width), F32),
                   jax.ShapeDtypeStruct((bsz, heads, seqlen, LANES), BF16),
                   jax.ShapeDtypeStruct((rows, width), BF16)],
        compiler_params=_params("parallel", "parallel", vmem_limit_bytes=DENSE_VMEM_LIMIT_BYTES),
        name="kv_proj_t",
    )(x, g_kv, g_q, wkt, wv, wq)


def _attn_sample_kernel(q_ref, ckt_ref, cv_ref, kn_ref, vn_ref, lam_ref, gsub_ref, o_ref, qs_sc, m_sc, l_sc, acc_sc,
                        *, lq, heads, past, lam_init):
    j = pl.program_id(1)

    @pl.when(j == 0)
    def _():
        for h in range(heads):
            qs_sc[h] = _stack_maps(q_ref[0, :, h * LANES:(h + 1) * LANES])
        m_sc[...] = jnp.full(m_sc.shape, NEG_INF, F32)
        l_sc[...] = jnp.zeros_like(l_sc)
        acc_sc[...] = jnp.zeros_like(acc_sc)

    hr = range(heads)

    def update(ss, vts):
        m_prev = [m_sc[h] for h in hr]
        m_new = [jnp.maximum(m_prev[h], jnp.max(ss[h], axis=1, keepdims=True)) for h in hr]
        alpha = [jnp.exp2(m_prev[h] - m_new[h]) for h in hr]
        ps = [jnp.exp2(ss[h] - m_new[h]) for h in hr]
        pv = [jnp.dot(ps[h].astype(BF16), vts[h], preferred_element_type=F32) for h in hr]
        for h in hr:
            l_sc[h] = alpha[h] * l_sc[h] + jnp.sum(ps[h], axis=1, keepdims=True)
            acc_sc[h] = alpha[h] * acc_sc[h] + pv[h]
            m_sc[h] = m_new[h]

    v_heads = pltpu.einshape("phd->hpd", cv_ref[0])
    update([jnp.dot(qs_sc[h], ckt_ref[0, h * LANES:(h + 1) * LANES, :].astype(BF16), preferred_element_type=F32)
            for h in hr],
           [v_heads[h].astype(BF16) for h in hr])

    @pl.when(j == pl.num_programs(1) - 1)
    def _():
        row = lax.broadcasted_iota(jnp.int32, (2 * lq, lq), 0)
        row = jnp.where(row >= lq, row - lq, row)
        col = lax.broadcasted_iota(jnp.int32, (2 * lq, lq), 1)
        mask = ((past + col) // CHUNK) <= ((past + row) // CHUNK)
        lam = _lambda(lam_ref, lam_init)
        hs = [slice(h * LANES, (h + 1) * LANES) for h in hr]
        update([jnp.where(mask, _dot_nt(qs_sc[h], kn_ref[0, :, hs[h]]), NEG_INF) for h in hr],
               [vn_ref[0, :, hs[h]] for h in hr])
        for h in hr:
            o = _finish_head(acc_sc[h], l_sc[h], lam, gsub_ref[:, hs[h]], lam_init, lq)
            o_ref[0, :, hs[h]] = o.astype(BF16)


def _attn_sample(q, cache_kt, cache_v, k_new, v_new, lam_vec, g_sub, lam_init, tkc):
    bsz, lq, width = q.shape
    past = cache_kt.shape[2]
    heads = width // LANES
    assert past % tkc == 0 and past % CHUNK == 0
    new_blk = pl.BlockSpec((1, lq, width), lambda b, j: (b, 0, 0))
    return pl.pallas_call(
        functools.partial(_attn_sample_kernel, lq=lq, heads=heads, past=past, lam_init=lam_init),
        grid=(bsz, past // tkc),
        in_specs=[new_blk,
                  pl.BlockSpec((1, width, tkc), lambda b, j: (b, 0, j)),
                  pl.BlockSpec((1, tkc, heads, LANES), lambda b, j: (b, j, 0, 0)),
                  new_blk,
                  new_blk,
                  pl.BlockSpec((4, HD_B), lambda b, j: (0, 0)),
                  pl.BlockSpec((1, width), lambda b, j: (0, 0))],
        out_specs=new_blk,
        out_shape=jax.ShapeDtypeStruct((bsz, lq, width), BF16),
        scratch_shapes=[pltpu.VMEM((heads, 2 * lq, LANES), BF16),
                        pltpu.VMEM((heads, 2 * lq, 1), F32),
                        pltpu.VMEM((heads, 2 * lq, 1), F32),
                        pltpu.VMEM((heads, 2 * lq, LANES), F32)],
        compiler_params=_params("parallel", "arbitrary"),
        name="attn_sample",
    )(q, cache_kt, cache_v, k_new, v_new, lam_vec, g_sub)


def _lambda_init(layer_idx):
    return 0.8 - 0.6 * math.exp(-0.3 * layer_idx)


def _trunk(x3, state, cache, p, mlstm_blk):
    bsz, seqlen, d = x3.shape
    rows = bsz * seqlen
    tm = min(ROW_TILE, rows)
    x = x3.reshape(rows, d)
    row = lambda g: g.reshape(1, -1)

    qk_w = H_A * p["dk"]
    v_w = H_A * p["dv"]
    w_in = p["w_in_a"]
    w_parts = [w_in[:, 0:qk_w], w_in[:, qk_w:2 * qk_w], w_in[:, 2 * qk_w:2 * qk_w + v_w],
               w_in[:, 2 * qk_w + v_w:2 * qk_w + 2 * v_w]]
    w_gate = jnp.pad(w_in[:, 2 * qk_w + 2 * v_w:], ((0, 0), (0, LANES - 2 * H_A)))
    ws = [w.astype(BF16) for w in w_parts] + [w_gate.astype(BF16)]
    specs = [(1.0, (BF16,)), (p["dk"] ** -0.5, (BF16,)), (1.0, (BF16,)), (1.0, (BF16,)), (1.0, (F32,))]
    q, k, v, o, gates = _norm_proj(x, row(p["g_mix_pre"][0]), ws, specs, min(DENSE_ROW_TILE, rows))

    bias = jnp.concatenate([p["b_i_a"], p["b_f_a"]]).astype(F32)
    b_col = bias.reshape(2 * H_A, 1)
    c0, n0, m0 = state
    m0b = jnp.broadcast_to(m0[:, :, None], (bsz, H_A, LANES))
    n0b = jnp.broadcast_to(n0[:, :, :, None], n0.shape + (LANES,))
    r3 = lambda a: a.reshape(bsz, seqlen, a.shape[-1])
    hg, c1, n1b, m1 = _mlstm(r3(q), r3(k), r3(v), r3(o), r3(gates), b_col, row(p["g_h_a"]), c0, n0b, m0b,
                             mlstm_blk, MLSTM_STREAMS, min(MLSTM_BLOCKS_PER_STEP, seqlen // mlstm_blk))
    n1 = n1b[:, :, :, 0]
    x = _mix_mlp(hg.reshape(rows, v_w), x, p["w_out_a"].astype(BF16), row(p["g_mix_post"][0]),
                 row(p["g_ffn_pre"][0]), p["w_up"][0].astype(BF16), p["w_down"][0].astype(BF16),
                 row(p["g_ffn_post"][0]), min(DENSE_ROW_TILE, rows), FF_TILE)

    kb_w = H_B * 2 * HD_B
    w_k = p["w_kv"][:, :kb_w]
    w_v = p["w_kv"][:, kb_w:].astype(BF16)
    q_scale = HD_B ** -0.5 * math.log2(math.e)
    w_q = p["w_q_b"].astype(BF16)
    g_q = row(p["g_mix_pre"][1])

    lam_init = _lambda_init(1)
    lam_vec = jnp.stack([p["lambda_q1"], p["lambda_k1"], p["lambda_q2"], p["lambda_k2"]]).astype(F32)
    g_sub = row(p["g_sub_b"])
    shape3 = (bsz, seqlen, kb_w)
    if cache is None:
        kt_new, kt_bf, v_new, v_bf, qb = _kv_proj_t(x, row(p["g_kv"]), g_q, w_k.T.astype(BF16), w_v, w_q, q_scale,
                                                    bsz, min(DENSE_ROW_TILE, seqlen), ATTN_TILE)
        k_out = kt_new.reshape(bsz, H_B, 2, HD_B, seqlen).transpose(0, 4, 1, 2, 3)
        att = _attn_prompt(qb.reshape(shape3), kt_bf, v_bf, lam_vec, g_sub, lam_init, ATTN_HEAD_GROUP,
                           min(ATTN_QUERY_TILES_PER_STEP, seqlen // ATTN_TILE))
    else:
        (qb,) = _norm_proj(x, g_q, [w_q], [(q_scale, (BF16,))], tm)
        k_new, k_bf, v_new, v_bf = _norm_proj(x, row(p["g_kv"]), [w_k.astype(BF16), w_v],
                                              [(1.0, (F32, BF16)), (1.0, (F32, BF16))], tm)
        k_out = k_new.reshape(bsz, seqlen, H_B, 2, HD_B)
        ck, cv = cache
        past = ck.shape[1]
        ckt = ck.transpose(0, 2, 3, 4, 1).reshape(bsz, kb_w, past)
        att = _attn_sample(qb.reshape(shape3), ckt, cv, k_bf.reshape(shape3), v_bf.reshape(shape3),
                           lam_vec, g_sub, lam_init, min(SAMPLE_KEY_TILE, past))
    x = _mix_mlp(att.reshape(rows, kb_w), x, p["w_o_b"].astype(BF16), row(p["g_mix_post"][1]),
                 row(p["g_ffn_pre"][1]), p["w_up"][1].astype(BF16), p["w_down"][1].astype(BF16),
                 row(p["g_ffn_post"][1]), min(DENSE_ROW_TILE, rows), FF_TILE)

    y = x.reshape(bsz, seqlen, d)
    v_out = v_new.reshape(bsz, seqlen, H_B, 2 * HD_B)
    return y, (c1[None], n1[None], m1[None, :, :, 0]), k_out, v_out


def kernel(x_prompt, x_sample, cache_k, cache_v, state_c, state_n, state_m, w_in_a, b_i_a, b_f_a, g_h_a, w_out_a,
           g_kv, w_kv, w_q_b, lambda_q1, lambda_k1, lambda_q2, lambda_k2, g_sub_b, w_o_b, g_mix_pre, g_mix_post,
           g_ffn_pre, g_ffn_post, w_up, w_down):
    assert w_in_a.shape[0] == 1 and w_q_b.shape[0] == 1 and g_mix_pre.shape[0] == 2
    dk, dv = state_c.shape[3], state_c.shape[4]
    p = {"dk": dk, "dv": dv,
         "w_in_a": w_in_a[0], "b_i_a": b_i_a[0], "b_f_a": b_f_a[0], "g_h_a": g_h_a[0], "w_out_a": w_out_a[0],
         "g_kv": g_kv, "w_kv": w_kv, "w_q_b": w_q_b[0],
         "lambda_q1": lambda_q1[0], "lambda_k1": lambda_k1[0], "lambda_q2": lambda_q2[0], "lambda_k2": lambda_k2[0],
         "g_sub_b": g_sub_b[0], "w_o_b": w_o_b[0], "g_mix_pre": g_mix_pre, "g_mix_post": g_mix_post,
         "g_ffn_pre": g_ffn_pre, "g_ffn_post": g_ffn_post, "w_up": w_up, "w_down": w_down}
    bsz = x_prompt.shape[0]
    zero_state = (jnp.zeros((bsz, H_A, dk, dv), F32), jnp.zeros((bsz, H_A, dk), F32), jnp.zeros((bsz, H_A), F32))
    y_p, (c_p, n_p, m_p), k_p, v_p = _trunk(x_prompt, zero_state, None, p, MLSTM_BLOCK)
    y_s, (c_s, n_s, m_s), k_s, v_s = _trunk(
        x_sample, (state_c[0].astype(F32), state_n[0].astype(F32), state_m[0].astype(F32)),
        (cache_k, cache_v), p, x_sample.shape[1])
    return (y_p, y_s, c_p, n_p, m_p, k_p, v_p, c_s, n_s, m_s, k_s, v_s)
```

```python
import functools
import math

import jax
import jax.numpy as jnp
from jax import lax
from jax.experimental import pallas as pl
from jax.experimental.pallas import tpu as pltpu

F32 = jnp.float32
BF16 = jnp.bfloat16

EPS = 1e-6
NEG_INF = -1e30
CHUNK = 64
LANES = 128
H_A = 8
H_B = 8
HD_B = 64
MLSTM_BLOCK = 128
MLSTM_HEAD_GROUP = 8
MLSTM_STREAMS = 1
MLSTM_BLOCKS_PER_STEP = 4
ATTN_TILE = 512
KEY_LOOP_UNROLL = 2
ATTN_HEAD_GROUP = 2
ATTN_QUERY_TILES_PER_STEP = 4
ROW_TILE = 512
DENSE_ROW_TILE = 1024
FF_TILE = 1024
SAMPLE_KEY_TILE = 1024
VMEM_LIMIT_BYTES = 48 * 1024 * 1024
DENSE_VMEM_LIMIT_BYTES = 58 * 1024 * 1024


def _params(*sem, vmem_limit_bytes=VMEM_LIMIT_BYTES):
    return pltpu.CompilerParams(dimension_semantics=sem, vmem_limit_bytes=vmem_limit_bytes)


def _rms(x, g):
    return x * lax.rsqrt(jnp.mean(x * x, axis=-1, keepdims=True) + EPS) * g


def _log_sigmoid(x):
    return jnp.minimum(x, 0.0) - jnp.log1p(jnp.exp(-jnp.abs(x)))


def _dot_nt(a, b):
    return lax.dot_general(a, b, (((1,), (1,)), ((), ())), preferred_element_type=F32)


def _dot_tn(a, b):
    return lax.dot_general(a, b, (((0,), (0,)), ((), ())), preferred_element_type=F32)


def _norm_proj_kernel(x_ref, g_ref, *refs, specs, parts):
    n_w = len(specs)
    w_refs, o_refs = refs[:n_w], refs[n_w:]
    rows = x_ref.shape[0] // parts
    rs = [slice(r * rows, (r + 1) * rows) for r in range(parts)]
    xn = [_rms(x_ref[s, :], g_ref[...]).astype(BF16) for s in rs]
    oi = 0
    for w_ref, (scale, dtypes) in zip(w_refs, specs):
        for r in range(parts):
            acc = jnp.dot(xn[r], w_ref[...], preferred_element_type=F32)
            if scale != 1.0:
                acc = acc * scale
            for k, dt in enumerate(dtypes):
                o_refs[oi + k][rs[r], :] = acc.astype(dt)
        oi += len(dtypes)


def _norm_proj(x, g, ws, specs, tm):
    m, d = x.shape
    assert m % tm == 0 and (tm <= ROW_TILE or tm % ROW_TILE == 0)
    parts = max(1, tm // ROW_TILE)
    out_shapes, out_specs = [], []
    for w, (_, dtypes) in zip(ws, specs):
        for dt in dtypes:
            out_shapes.append(jax.ShapeDtypeStruct((m, w.shape[1]), dt))
            out_specs.append(pl.BlockSpec((tm, w.shape[1]), lambda i: (i, 0)))
    in_specs = [pl.BlockSpec((tm, d), lambda i: (i, 0)), pl.BlockSpec((1, d), lambda i: (0, 0))]
    in_specs += [pl.BlockSpec(w.shape, lambda i: (0, 0)) for w in ws]
    return pl.pallas_call(
        functools.partial(_norm_proj_kernel, specs=tuple(specs), parts=parts),
        grid=(m // tm,),
        in_specs=in_specs,
        out_specs=out_specs,
        out_shape=out_shapes,
        compiler_params=_params("parallel"),
        name="norm_proj",
    )(x, g, *ws)


def _mix_mlp_kernel(h_ref, x_ref, wo_ref, gmix_ref, gpre_ref, wup_ref, wdn_ref, gpost_ref, o_ref, *, tf, parts):
    rows = h_ref.shape[0] // parts
    rs = [slice(r * rows, (r + 1) * rows) for r in range(parts)]
    x1 = [x_ref[s, :] + _rms(jnp.dot(h_ref[s, :], wo_ref[...], preferred_element_type=F32), gmix_ref[...])
          for s in rs]
    xn = [_rms(x, gpre_ref[...]).astype(BF16) for x in x1]
    acc = [None] * parts
    for c in range(wup_ref.shape[1] // tf):
        for r in range(parts):
            up = jnp.maximum(jnp.dot(xn[r], wup_ref[:, c * tf:(c + 1) * tf], preferred_element_type=F32), 0.0)
            part = jnp.dot((up * up).astype(BF16), wdn_ref[c * tf:(c + 1) * tf, :], preferred_element_type=F32)
            acc[r] = part if acc[r] is None else acc[r] + part
    for r in range(parts):
        o_ref[rs[r], :] = x1[r] + _rms(acc[r], gpost_ref[...])


def _mix_mlp(h, x, w_o, g_mix, g_pre, w_up, w_down, g_post, tm, tf):
    m, d = x.shape
    k = h.shape[1]
    ff = w_up.shape[1]
    assert m % tm == 0 and ff % tf == 0 and (tm <= ROW_TILE or tm % ROW_TILE == 0)
    resident = lambda shape: pl.BlockSpec(shape, lambda i: (0, 0), pipeline_mode=pl.Buffered(1))
    return pl.pallas_call(
        functools.partial(_mix_mlp_kernel, tf=tf, parts=max(1, tm // ROW_TILE)),
        grid=(m // tm,),
        in_specs=[pl.BlockSpec((tm, k), lambda i: (i, 0)),
                  pl.BlockSpec((tm, d), lambda i: (i, 0)),
                  resident((k, d)),
                  resident((1, d)),
                  resident((1, d)),
                  resident((d, ff)),
                  resident((ff, d)),
                  resident((1, d))],
        out_specs=pl.BlockSpec((tm, d), lambda i: (i, 0)),
        out_shape=jax.ShapeDtypeStruct((m, d), F32),
        compiler_params=_params("parallel", vmem_limit_bytes=DENSE_VMEM_LIMIT_BYTES),
        name="mix_mlp",
    )(h, x, w_o, g_mix, g_pre, w_up, w_down, g_post)


def _mlstm_kernel(q_ref, k_ref, v_ref, o_ref, gate_ref, bcol_ref, gh_ref, c0_ref, n0_ref, m0_ref,
                  hg_ref, c_ref, n_ref, m_ref, *, nb, blk, sub, heads, dk, dv):
    t = pl.program_id(1)

    @pl.when(t == 0)
    def _():
        c_ref[...] = c0_ref[...]
        n_ref[...] = n0_ref[...]
        m_ref[...] = m0_ref[...]

    row = lax.broadcasted_iota(jnp.int32, (blk, blk), 0)
    col = lax.broadcasted_iota(jnp.int32, (blk, blk), 1)
    causal = row >= col
    triu = (row <= col).astype(F32)

    def square(a):
        if a.shape[1] < LANES:
            a = jnp.concatenate([a, jnp.zeros((a.shape[0], LANES - a.shape[1]), F32)], axis=1)
        if a.shape[0] < LANES:
            a = jnp.concatenate([a, jnp.zeros((LANES - a.shape[0], a.shape[1]), F32)], axis=0)
        return a

    all_pairs = [(b, h) for b in range(nb) for h in range(heads)]

    def one_block(rows):
        for g0 in range(0, len(all_pairs), MLSTM_HEAD_GROUP):
            _mlstm_group(all_pairs[g0:g0 + MLSTM_HEAD_GROUP], rows, q_ref, k_ref, v_ref, o_ref, gate_ref, bcol_ref,
                         gh_ref, hg_ref, c_ref, n_ref, m_ref, causal, triu, square,
                         blk=blk, heads=heads, dk=dk, dv=dv)

    if sub == 1:
        one_block(slice(None))
    else:
        def body(j, carry):
            one_block(pl.ds(pl.multiple_of(j * blk, blk), blk))
            return carry

        lax.fori_loop(0, sub, body, 0)


def _mlstm_group(pairs, rows, q_ref, k_ref, v_ref, o_ref, gate_ref, bcol_ref, gh_ref, hg_ref, c_ref, n_ref, m_ref,
                 causal, triu, square, *, blk, heads, dk, dv):
    hr = range(len(pairs))
    streams = sorted({b for b, _ in pairs})
    qs = [q_ref[b, rows, h * dk:(h + 1) * dk] for b, h in pairs]
    ks = [k_ref[b, rows, h * dk:(h + 1) * dk] for b, h in pairs]
    vs = [v_ref[b, rows, h * dv:(h + 1) * dv] for b, h in pairs]
    c0s = [c_ref[b, h] for b, h in pairs]
    n0s = [n_ref[b, h] for b, h in pairs]
    m0s = [m_ref[b, h:h + 1, :] for b, h in pairs]
    ones = jnp.ones((blk, LANES), BF16)
    v1s = [jnp.concatenate([vs[h], ones], axis=1) for h in hr]
    cns = [jnp.concatenate([c0s[h], n0s[h]], axis=1).astype(BF16) for h in hr]

    qk = [_dot_nt(qs[h], ks[h]) for h in hr]
    qcn = [jnp.dot(qs[h], cns[h], preferred_element_type=F32) for h in hr]

    stats_cols, r_rows = {}, {}
    for b in streams:
        gates_row = square(gate_ref[b, rows, :]).T[:, :blk]
        z_row = gates_row[0:2 * heads, :] + bcol_ref[...]
        li_row = z_row[0:heads, :]
        lf_row = _log_sigmoid(z_row[heads:2 * heads, :])
        fcum_row = jnp.dot(lf_row, triu, preferred_element_type=F32, precision=lax.Precision.HIGHEST)
        r_rows[b] = li_row - fcum_row
        stats_cols[b] = square(jnp.concatenate([li_row, fcum_row], axis=0)).T[:blk, :]
    r_row = [r_rows[b][h:h + 1, :] for b, h in pairs]
    fcs = [jnp.broadcast_to(stats_cols[b][:, heads + h:heads + h + 1], (blk, LANES)) for b, h in pairs]
    lis = [jnp.broadcast_to(stats_cols[b][:, h:h + 1], (blk, LANES)) for b, h in pairs]

    def wide(x):
        return x[:, :blk] if blk <= LANES else jnp.concatenate([x] * (blk // LANES), axis=1)

    dmat = [jnp.where(causal, wide(fcs[h]) + r_row[h], NEG_INF) for h in hr]
    a = [fcs[h] + m0s[h] for h in hr]
    m_new = [jnp.maximum(a[h], jnp.max(dmat[h], axis=1, keepdims=True)) for h in hr]
    s = [qk[h] * jnp.exp(dmat[h] - wide(m_new[h])) for h in hr]
    inter = [jnp.exp(a[h] - m_new[h]) for h in hr]
    sv1 = [jnp.dot(s[h].astype(BF16), v1s[h], preferred_element_type=F32) for h in hr]
    qn = [sv1[h][:, dv:] + inter[h] * qcn[h][:, dv:] for h in hr]
    hh = [(sv1[h][:, :dv] + inter[h] * qcn[h][:, :dv]) / jnp.maximum(jnp.abs(qn[h]), jnp.exp(-m_new[h]))
          for h in hr]

    for i, (b, h) in enumerate(pairs):
        hn = _rms(hh[i], gh_ref[:, h * dv:(h + 1) * dv])
        og = o_ref[b, rows, h * dv:(h + 1) * dv].astype(F32)
        hg_ref[b, rows, h * dv:(h + 1) * dv] = (hn * jax.nn.sigmoid(og)).astype(BF16)

    m_last = [m_new[h][blk - 1:blk, :] for h in hr]
    w_last = [jnp.exp(fcs[h][blk - 1:blk, :] - fcs[h] + lis[h] - m_last[h]) for h in hr]
    decay = [jnp.exp(a[h][blk - 1:blk, :] - m_last[h]) for h in hr]
    kw = [ks[h].astype(F32) * w_last[h][:, :dk] for h in hr]
    kv1 = [_dot_tn(kw[h].astype(BF16), v1s[h]) for h in hr]
    for i, (b, h) in enumerate(pairs):
        c_ref[b, h] = decay[i] * c0s[i] + kv1[i][:, :dv]
        n_ref[b, h] = decay[i] * n0s[i] + kv1[i][:, dv:]
        m_ref[b, h:h + 1, :] = m_last[i]


def _mlstm(q, k, v, o, gates, b_col, g_h, c0, n0, m0, blk, nb, sub):
    bsz, seqlen, _ = q.shape
    heads, dk, dv = c0.shape[1], c0.shape[2], c0.shape[3]
    step = blk * sub
    assert seqlen % step == 0 and bsz % nb == 0 and dv == LANES
    row_map = lambda b, t: (b, t, 0)
    const2 = lambda b, t: (0, 0)
    st4 = lambda b, t: (b, 0, 0, 0)
    st3 = lambda b, t: (b, 0, 0)
    return pl.pallas_call(
        functools.partial(_mlstm_kernel, nb=nb, blk=blk, sub=sub, heads=heads, dk=dk, dv=dv),
        grid=(bsz // nb, seqlen // step),
        in_specs=[pl.BlockSpec((nb, step, heads * dk), row_map),
                  pl.BlockSpec((nb, step, heads * dk), row_map),
                  pl.BlockSpec((nb, step, heads * dv), row_map),
                  pl.BlockSpec((nb, step, heads * dv), row_map),
                  pl.BlockSpec((nb, step, LANES), row_map),
                  pl.BlockSpec((2 * heads, 1), const2),
                  pl.BlockSpec((1, heads * dv), const2),
                  pl.BlockSpec((nb, heads, dk, dv), st4),
                  pl.BlockSpec((nb, heads, dk, LANES), st4),
                  pl.BlockSpec((nb, heads, LANES), st3)],
        out_specs=[pl.BlockSpec((nb, step, heads * dv), row_map),
                   pl.BlockSpec((nb, heads, dk, dv), st4),
                   pl.BlockSpec((nb, heads, dk, LANES), st4),
                   pl.BlockSpec((nb, heads, LANES), st3)],
        out_shape=[jax.ShapeDtypeStruct((bsz, seqlen, heads * dv), BF16),
                   jax.ShapeDtypeStruct((bsz, heads, dk, dv), F32),
                   jax.ShapeDtypeStruct((bsz, heads, dk, LANES), F32),
                   jax.ShapeDtypeStruct((bsz, heads, LANES), F32)],
        compiler_params=_params("parallel", "arbitrary"),
        name="mlstm",
    )(q, k, v, o, gates, b_col, g_h, c0, n0, m0)


def _stack_maps(q):
    lane = lax.broadcasted_iota(jnp.int32, q.shape, 1)
    zero = jnp.zeros_like(q)
    return jnp.concatenate([jnp.where(lane < HD_B, q, zero), jnp.where(lane >= HD_B, q, zero)], axis=0)


def _lambda(lam_ref, lam_init):
    lv = lam_ref[...]
    a = jnp.sum(lv[0:1] * lv[1:2], axis=1, keepdims=True)
    b = jnp.sum(lv[2:3] * lv[3:4], axis=1, keepdims=True)
    return jnp.exp(a) - jnp.exp(b) + lam_init


def _finish_head(acc, l, lam, g_sub, lam_init, tq):
    w = acc / l
    o = w[0:tq] - lam * w[tq:2 * tq]
    return _rms(o, g_sub) * (1.0 - lam_init)


def _attn_prompt_kernel(q_ref, kt_ref, v_ref, lam_ref, gsub_ref, o_ref, qs_sc, m_sc, l_sc, acc_sc,
                        *, tile, hg, qt, lam_init):
    first = pl.program_id(2) * qt
    lam = _lambda(lam_ref, lam_init)

    def one_query_tile(t, carry):
        _attn_query_tile(first + t, pl.ds(pl.multiple_of(t * tile, tile), tile), q_ref, kt_ref, v_ref, gsub_ref, o_ref,
                         qs_sc, m_sc, l_sc, acc_sc, lam, tile=tile, hg=hg, lam_init=lam_init)
        return carry

    lax.fori_loop(0, qt, one_query_tile, 0)


def _attn_query_tile(i, rows, q_ref, kt_ref, v_ref, gsub_ref, o_ref, qs_sc, m_sc, l_sc, acc_sc, lam,
                     *, tile, hg, lam_init):
    heads = range(hg)
    for g in heads:
        qs_sc[g] = _stack_maps(q_ref[0, rows, g * LANES:(g + 1) * LANES])
    m_sc[...] = jnp.full(m_sc.shape, NEG_INF, F32)
    l_sc[...] = jnp.zeros_like(l_sc)
    acc_sc[...] = jnp.zeros_like(acc_sc)
    n_lane_tiles = tile // LANES

    def step(j, masked):
        ss = [jnp.dot(qs_sc[g], kt_ref[0, j, g * LANES:(g + 1) * LANES, :], preferred_element_type=F32).astype(BF16)
              for g in heads]
        if masked:
            row = lax.broadcasted_iota(jnp.int32, ss[0].shape, 0) & (tile - 1)
            col = lax.broadcasted_iota(jnp.int32, ss[0].shape, 1)
            ss = [jnp.where((col // CHUNK) <= (row // CHUNK), s, NEG_INF).astype(BF16) for s in ss]
        start = pl.multiple_of(j * tile, tile)
        m_prev, m_new, alpha, ps, pv = [], [], [], [], []
        for g in heads:
            m_prev.append(m_sc[g])
            m_new.append(jnp.maximum(m_prev[g], jnp.max(ss[g], axis=1, keepdims=True).astype(F32)))
            alpha.append(jnp.exp2(m_prev[g] - m_new[g]))
            ps.append(jnp.exp2(ss[g] - jnp.concatenate([m_new[g].astype(BF16)] * n_lane_tiles, axis=1)))
            pv.append(jnp.dot(ps[g], v_ref[0, g, pl.ds(start, tile), :], preferred_element_type=F32))
        for g in heads:
            p_sum = ps[g][:, 0:LANES]
            for c in range(1, n_lane_tiles):
                p_sum = p_sum + ps[g][:, c * LANES:(c + 1) * LANES]
            l_sc[g] = alpha[g] * l_sc[g] + p_sum.astype(F32)
            acc_sc[g] = alpha[g] * acc_sc[g] + pv[g]
            m_sc[g] = m_new[g]

    def body(jj, carry):
        for u in range(KEY_LOOP_UNROLL):
            step(KEY_LOOP_UNROLL * jj + u, False)
        return carry

    n_full = i // KEY_LOOP_UNROLL
    lax.fori_loop(0, n_full, body, 0)
    for rem in range(1, KEY_LOOP_UNROLL):
        @pl.when(i - KEY_LOOP_UNROLL * n_full >= rem)
        def _():
            step(KEY_LOOP_UNROLL * n_full + rem - 1, False)
    step(i, True)
    for g in heads:
        l = jnp.sum(l_sc[g], axis=1, keepdims=True)
        o = _finish_head(acc_sc[g], l, lam, gsub_ref[:, g * LANES:(g + 1) * LANES], lam_init, tile)
        o_ref[0, rows, g * LANES:(g + 1) * LANES] = o.astype(BF16)


def _attn_prompt(q, kt, v, lam_vec, g_sub, lam_init, hg, qt):
    bsz, seqlen, width = q.shape
    tile = kt.shape[3]
    heads = width // LANES
    gw = hg * LANES
    assert seqlen % (tile * qt) == 0 and tile % CHUNK == 0 and (tile & (tile - 1)) == 0 and tile % LANES == 0
    assert heads % hg == 0
    return pl.pallas_call(
        functools.partial(_attn_prompt_kernel, tile=tile, hg=hg, qt=qt, lam_init=lam_init),
        grid=(bsz, heads // hg, seqlen // (tile * qt)),
        in_specs=[pl.BlockSpec((1, tile * qt, gw), lambda b, h, i: (b, i, h)),
                  pl.BlockSpec((1, seqlen // tile, gw, tile), lambda b, h, i: (b, 0, h, 0)),
                  pl.BlockSpec((1, hg, seqlen, LANES), lambda b, h, i: (b, h, 0, 0)),
                  pl.BlockSpec((4, HD_B), lambda b, h, i: (0, 0)),
                  pl.BlockSpec((1, gw), lambda b, h, i: (0, h))],
        out_specs=pl.BlockSpec((1, tile * qt, gw), lambda b, h, i: (b, i, h)),
        out_shape=jax.ShapeDtypeStruct((bsz, seqlen, width), BF16),
        scratch_shapes=[pltpu.VMEM((hg, 2 * tile, LANES), BF16),
                        pltpu.VMEM((hg, 2 * tile, LANES), F32),
                        pltpu.VMEM((hg, 2 * tile, LANES), F32),
                        pltpu.VMEM((hg, 2 * tile, LANES), F32)],
        compiler_params=_params("parallel", "parallel", "arbitrary"),
        name="attn_prompt",
    )(q, kt, v, lam_vec, g_sub)


def _kv_proj_t_kernel(x_ref, g_ref, gq_ref, wkt_ref, wv_ref, wq_ref, kt_ref, ktb_ref, v_ref, vb_ref, q_ref,
                      *, heads, q_scale, parts):
    rows = x_ref.shape[0] // parts
    rs = [slice(r * rows, (r + 1) * rows) for r in range(parts)]
    xs = [x_ref[s, :] for s in rs]
    xr = [x * lax.rsqrt(jnp.mean(x * x, axis=-1, keepdims=True) + EPS) for x in xs]
    xq = [(x * gq_ref[...]).astype(BF16) for x in xr]
    xn = [(x * g_ref[...]).astype(BF16) for x in xr]
    for r in range(parts):
        q = jnp.dot(xq[r], wq_ref[...], preferred_element_type=F32)
        q_ref[rs[r], :] = (q * q_scale).astype(BF16)
    for r in range(parts):
        kt = _dot_nt(wkt_ref[...], xn[r])
        kt_ref[0, :, rs[r]] = kt
        ktb_ref[0, r] = kt.astype(BF16)
    for r in range(parts):
        v = jnp.dot(xn[r], wv_ref[...], preferred_element_type=F32)
        v_ref[rs[r], :] = v
        for h in range(heads):
            vb_ref[0, h, rs[r], :] = v[:, h * LANES:(h + 1) * LANES].astype(BF16)


def _kv_proj_t(x, g_kv, g_q, wkt, wv, wq, q_scale, bsz, tm, kt_tile):
    rows, d = x.shape
    seqlen = rows // bsz
    width = wv.shape[1]
    heads = width // LANES
    assert seqlen % tm == 0 and tm % kt_tile == 0
    nt = seqlen // tm
    parts = tm // kt_tile
    row_blk = lambda b, i: (b * nt + i, 0)
    resident = lambda shape: pl.BlockSpec(shape, lambda b, i: (0, 0), pipeline_mode=pl.Buffered(1))
    return pl.pallas_call(
        functools.partial(_kv_proj_t_kernel, heads=heads, q_scale=q_scale, parts=parts),
        grid=(bsz, nt),
        in_specs=[pl.BlockSpec((tm, d), row_blk),
                  resident((1, d)),
                  resident((1, d)),
                  resident((width, d)),
                  resident((d, width)),
                  resident((d, width))],
        out_specs=[pl.BlockSpec((1, width, tm), lambda b, i: (b, 0, i)),
                   pl.BlockSpec((1, parts, width, kt_tile), lambda b, i: (b, i, 0, 0)),
                   pl.BlockSpec((tm, width), row_blk),
                   pl.BlockSpec((1, heads, tm, LANES), lambda b, i: (b, 0, i, 0)),
                   pl.BlockSpec((tm, width), row_blk)],
        out_shape=[jax.ShapeDtypeStruct((bsz, width, seqlen), F32),
                   jax.ShapeDtypeStruct((bsz, seqlen // kt_tile, width, kt_tile), BF16),
                   jax.ShapeDtypeStruct((rows, width), F32),
                   jax.ShapeDtypeStruct((bsz, heads, seqlen, LANES), BF16),
                   jax.ShapeDtypeStruct((rows, width), BF16)],
        compiler_params=_params("parallel", "parallel", vmem_limit_bytes=DENSE_VMEM_LIMIT_BYTES),
        name="kv_proj_t",
    )(x, g_kv, g_q, wkt, wv, wq)


def _attn_sample_kernel(q_ref, ckt_ref, cv_ref, kn_ref, vn_ref, lam_ref, gsub_ref, o_ref, qs_sc, m_sc, l_sc, acc_sc,
                        *, lq, heads, past, lam_init):
    j = pl.program_id(1)

    @pl.when(j == 0)
    def _():
        for h in range(heads):
            qs_sc[h] = _stack_maps(q_ref[0, :, h * LANES:(h + 1) * LANES])
        m_sc[...] = jnp.full(m_sc.shape, NEG_INF, F32)
        l_sc[...] = jnp.zeros_like(l_sc)
        acc_sc[...] = jnp.zeros_like(acc_sc)

    hr = range(heads)

    def update(ss, vts):
        m_prev = [m_sc[h] for h in hr]
        m_new = [jnp.maximum(m_prev[h], jnp.max(ss[h], axis=1, keepdims=True)) for h in hr]
        alpha = [jnp.exp2(m_prev[h] - m_new[h]) for h in hr]
        ps = [jnp.exp2(ss[h] - m_new[h]) for h in hr]
        pv = [jnp.dot(ps[h].astype(BF16), vts[h], preferred_element_type=F32) for h in hr]
        for h in hr:
            l_sc[h] = alpha[h] * l_sc[h] + jnp.sum(ps[h], axis=1, keepdims=True)
            acc_sc[h] = alpha[h] * acc_sc[h] + pv[h]
            m_sc[h] = m_new[h]

    v_heads = pltpu.einshape("phd->hpd", cv_ref[0])
    update([jnp.dot(qs_sc[h], ckt_ref[0, h * LANES:(h + 1) * LANES, :].astype(BF16), preferred_element_type=F32)
            for h in hr],
           [v_heads[h].astype(BF16) for h in hr])

    @pl.when(j == pl.num_programs(1) - 1)
    def _():
        row = lax.broadcasted_iota(jnp.int32, (2 * lq, lq), 0)
        row = jnp.where(row >= lq, row - lq, row)
        col = lax.broadcasted_iota(jnp.int32, (2 * lq, lq), 1)
        mask = ((past + col) // CHUNK) <= ((past + row) // CHUNK)
        lam = _lambda(lam_ref, lam_init)
        hs = [slice(h * LANES, (h + 1) * LANES) for h in hr]
        update([jnp.where(mask, _dot_nt(qs_sc[h], kn_ref[0, :, hs[h]]), NEG_INF) for h in hr],
               [vn_ref[0, :, hs[h]] for h in hr])
        for h in hr:
            o = _finish_head(acc_sc[h], l_sc[h], lam, gsub_ref[:, hs[h]], lam_init, lq)
            o_ref[0, :, hs[h]] = o.astype(BF16)


def _attn_sample(q, cache_kt, cache_v, k_new, v_new, lam_vec, g_sub, lam_init, tkc):
    bsz, lq, width = q.shape
    past = cache_kt.shape[2]
    heads = width // LANES
    assert past % tkc == 0 and past % CHUNK == 0
    new_blk = pl.BlockSpec((1, lq, width), lambda b, j: (b, 0, 0))
    return pl.pallas_call(
        functools.partial(_attn_sample_kernel, lq=lq, heads=heads, past=past, lam_init=lam_init),
        grid=(bsz, past // tkc),
        in_specs=[new_blk,
                  pl.BlockSpec((1, width, tkc), lambda b, j: (b, 0, j)),
                  pl.BlockSpec((1, tkc, heads, LANES), lambda b, j: (b, j, 0, 0)),
                  new_blk,
                  new_blk,
                  pl.BlockSpec((4, HD_B), lambda b, j: (0, 0)),
                  pl.BlockSpec((1, width), lambda b, j: (0, 0))],
        out_specs=new_blk,
        out_shape=jax.ShapeDtypeStruct((bsz, lq, width), BF16),
        scratch_shapes=[pltpu.VMEM((heads, 2 * lq, LANES), BF16),
                        pltpu.VMEM((heads, 2 * lq, 1), F32),
                        pltpu.VMEM((heads, 2 * lq, 1), F32),
                        pltpu.VMEM((heads, 2 * lq, LANES), F32)],
        compiler_params=_params("parallel", "arbitrary"),
        name="attn_sample",
    )(q, cache_kt, cache_v, k_new, v_new, lam_vec, g_sub)


def _lambda_init(layer_idx):
    return 0.8 - 0.6 * math.exp(-0.3 * layer_idx)


def _trunk(x3, state, cache, p, mlstm_blk):
    bsz, seqlen, d = x3.shape
    rows = bsz * seqlen
    tm = min(ROW_TILE, rows)
    x = x3.reshape(rows, d)
    row = lambda g: g.reshape(1, -1)

    qk_w = H_A * p["dk"]
    v_w = H_A * p["dv"]
    w_in = p["w_in_a"]
    w_parts = [w_in[:, 0:qk_w], w_in[:, qk_w:2 * qk_w], w_in[:, 2 * qk_w:2 * qk_w + v_w],
               w_in[:, 2 * qk_w + v_w:2 * qk_w + 2 * v_w]]
    w_gate = jnp.pad(w_in[:, 2 * qk_w + 2 * v_w:], ((0, 0), (0, LANES - 2 * H_A)))
    ws = [w.astype(BF16) for w in w_parts] + [w_gate.astype(BF16)]
    specs = [(1.0, (BF16,)), (p["dk"] ** -0.5, (BF16,)), (1.0, (BF16,)), (1.0, (BF16,)), (1.0, (F32,))]
    q, k, v, o, gates = _norm_proj(x, row(p["g_mix_pre"][0]), ws, specs, min(DENSE_ROW_TILE, rows))

    bias = jnp.concatenate([p["b_i_a"], p["b_f_a"]]).astype(F32)
    b_col = bias.reshape(2 * H_A, 1)
    c0, n0, m0 = state
    m0b = jnp.broadcast_to(m0[:, :, None], (bsz, H_A, LANES))
    n0b = jnp.broadcast_to(n0[:, :, :, None], n0.shape + (LANES,))
    r3 = lambda a: a.reshape(bsz, seqlen, a.shape[-1])
    hg, c1, n1b, m1 = _mlstm(r3(q), r3(k), r3(v), r3(o), r3(gates), b_col, row(p["g_h_a"]), c0, n0b, m0b,
                             mlstm_blk, MLSTM_STREAMS, min(MLSTM_BLOCKS_PER_STEP, seqlen // mlstm_blk))
    n1 = n1b[:, :, :, 0]
    x = _mix_mlp(hg.reshape(rows, v_w), x, p["w_out_a"].astype(BF16), row(p["g_mix_post"][0]),
                 row(p["g_ffn_pre"][0]), p["w_up"][0].astype(BF16), p["w_down"][0].astype(BF16),
                 row(p["g_ffn_post"][0]), min(DENSE_ROW_TILE, rows), FF_TILE)

    kb_w = H_B * 2 * HD_B
    w_k = p["w_kv"][:, :kb_w]
    w_v = p["w_kv"][:, kb_w:].astype(BF16)
    q_scale = HD_B ** -0.5 * math.log2(math.e)
    w_q = p["w_q_b"].astype(BF16)
    g_q = row(p["g_mix_pre"][1])

    lam_init = _lambda_init(1)
    lam_vec = jnp.stack([p["lambda_q1"], p["lambda_k1"], p["lambda_q2"], p["lambda_k2"]]).astype(F32)
    g_sub = row(p["g_sub_b"])
    shape3 = (bsz, seqlen, kb_w)
    if cache is None:
        kt_new, kt_bf, v_new, v_bf, qb = _kv_proj_t(x, row(p["g_kv"]), g_q, w_k.T.astype(BF16), w_v, w_q, q_scale,
                                                    bsz, min(DENSE_ROW_TILE, seqlen), ATTN_TILE)
        k_out = kt_new.reshape(bsz, H_B, 2, HD_B, seqlen).transpose(0, 4, 1, 2, 3)
        att = _attn_prompt(qb.reshape(shape3), kt_bf, v_bf, lam_vec, g_sub, lam_init, ATTN_HEAD_GROUP,
                           min(ATTN_QUERY_TILES_PER_STEP, seqlen // ATTN_TILE))
    else:
        (qb,) = _norm_proj(x, g_q, [w_q], [(q_scale, (BF16,))], tm)
        k_new, k_bf, v_new, v_bf = _norm_proj(x, row(p["g_kv"]), [w_k.astype(BF16), w_v],
                                              [(1.0, (F32, BF16)), (1.0, (F32, BF16))], tm)
        k_out = k_new.reshape(bsz, seqlen, H_B, 2, HD_B)
        ck, cv = cache
        past = ck.shape[1]
        ckt = ck.transpose(0, 2, 3, 4, 1).reshape(bsz, kb_w, past)
        att = _attn_sample(qb.reshape(shape3), ckt, cv, k_bf.reshape(shape3), v_bf.reshape(shape3),
                           lam_vec, g_sub, lam_init, min(SAMPLE_KEY_TILE, past))
    x = _mix_mlp(att.reshape(rows, kb_w), x, p["w_o_b"].astype(BF16), row(p["g_mix_post"][1]),
                 row(p["g_ffn_pre"][1]), p["w_up"][1].astype(BF16), p["w_down"][1].astype(BF16),
                 row(p["g_ffn_post"][1]), min(DENSE_ROW_TILE, rows), FF_TILE)

    y = x.reshape(bsz, seqlen, d)
    v_out = v_new.reshape(bsz, seqlen, H_B, 2 * HD_B)
    return y, (c1[None], n1[None], m1[None, :, :, 0]), k_out, v_out


def kernel(x_prompt, x_sample, cache_k, cache_v, state_c, state_n, state_m, w_in_a, b_i_a, b_f_a, g_h_a, w_out_a,
           g_kv, w_kv, w_q_b, lambda_q1, lambda_k1, lambda_q2, lambda_k2, g_sub_b, w_o_b, g_mix_pre, g_mix_post,
           g_ffn_pre, g_ffn_post, w_up, w_down):
    assert w_in_a.shape[0] == 1 and w_q_b.shape[0] == 1 and g_mix_pre.shape[0] == 2
    dk, dv = state_c.shape[3], state_c.shape[4]
    p = {"dk": dk, "dv": dv,
         "w_in_a": w_in_a[0], "b_i_a": b_i_a[0], "b_f_a": b_f_a[0], "g_h_a": g_h_a[0], "w_out_a": w_out_a[0],
         "g_kv": g_kv, "w_kv": w_kv, "w_q_b": w_q_b[0],
         "lambda_q1": lambda_q1[0], "lambda_k1": lambda_k1[0], "lambda_q2": lambda_q2[0], "lambda_k2": lambda_k2[0],
         "g_sub_b": g_sub_b[0], "w_o_b": w_o_b[0], "g_mix_pre": g_mix_pre, "g_mix_post": g_mix_post,
         "g_ffn_pre": g_ffn_pre, "g_ffn_post": g_ffn_post, "w_up": w_up, "w_down": w_down}
    bsz = x_prompt.shape[0]
    zero_state = (jnp.zeros((bsz, H_A, dk, dv), F32), jnp.zeros((bsz, H_A, dk), F32), jnp.zeros((bsz, H_A), F32))
    y_p, (c_p, n_p, m_p), k_p, v_p = _trunk(x_prompt, zero_state, None, p, MLSTM_BLOCK)
    y_s, (c_s, n_s, m_s), k_s, v_s = _trunk(
        x_sample, (state_c[0].astype(F32), state_n[0].astype(F32), state_m[0].astype(F32)),
        (cache_k, cache_v), p, x_sample.shape[1])
    return (y_p, y_s, c_p, n_p, m_p, k_p, v_p, c_s, n_s, m_s, k_s, v_s)
```

```python
import functools
import math

import jax
import jax.numpy as jnp
from jax import lax
from jax.experimental import pallas as pl
from jax.experimental.pallas import tpu as pltpu

F32 = jnp.float32
BF16 = jnp.bfloat16

EPS = 1e-6
NEG_INF = -1e30
CHUNK = 64
LANES = 128
H_A = 8
H_B = 8
HD_B = 64
MLSTM_BLOCK = 128
MLSTM_HEAD_GROUP = 8
MLSTM_STREAMS = 1
MLSTM_BLOCKS_PER_STEP = 4
ATTN_TILE = 512
KEY_LOOP_UNROLL = 2
ATTN_HEAD_GROUP = 2
ATTN_QUERY_TILES_PER_STEP = 4
ROW_TILE = 512
DENSE_ROW_TILE = 1024
FF_TILE = 1024
SAMPLE_KEY_TILE = 1024
VMEM_LIMIT_BYTES = 48 * 1024 * 1024
DENSE_VMEM_LIMIT_BYTES = 58 * 1024 * 1024


def _params(*sem, vmem_limit_bytes=VMEM_LIMIT_BYTES):
    return pltpu.CompilerParams(dimension_semantics=sem, vmem_limit_bytes=vmem_limit_bytes)


def _rms(x, g):
    return x * lax.rsqrt(jnp.mean(x * x, axis=-1, keepdims=True) + EPS) * g


def _log_sigmoid(x):
    return jnp.minimum(x, 0.0) - jnp.log1p(jnp.exp(-jnp.abs(x)))


def _dot_nt(a, b):
    return lax.dot_general(a, b, (((1,), (1,)), ((), ())), preferred_element_type=F32)


def _dot_tn(a, b):
    return lax.dot_general(a, b, (((0,), (0,)), ((), ())), preferred_element_type=F32)


def _norm_proj_kernel(x_ref, g_ref, *refs, specs, parts):
    n_w = len(specs)
    w_refs, o_refs = refs[:n_w], refs[n_w:]
    rows = x_ref.shape[0] // parts
    rs = [slice(r * rows, (r + 1) * rows) for r in range(parts)]
    xn = [_rms(x_ref[s, :], g_ref[...]).astype(BF16) for s in rs]
    oi = 0
    for w_ref, (scale, dtypes) in zip(w_refs, specs):
        for r in range(parts):
            acc = jnp.dot(xn[r], w_ref[...], preferred_element_type=F32)
            if scale != 1.0:
                acc = acc * scale
            for k, dt in enumerate(dtypes):
                o_refs[oi + k][rs[r], :] = acc.astype(dt)
        oi += len(dtypes)


def _norm_proj(x, g, ws, specs, tm):
    m, d = x.shape
    assert m % tm == 0 and (tm <= ROW_TILE or tm % ROW_TILE == 0)
    parts = max(1, tm // ROW_TILE)
    out_shapes, out_specs = [], []
    for w, (_, dtypes) in zip(ws, specs):
        for dt in dtypes:
            out_shapes.append(jax.ShapeDtypeStruct((m, w.shape[1]), dt))
            out_specs.append(pl.BlockSpec((tm, w.shape[1]), lambda i: (i, 0)))
    in_specs = [pl.BlockSpec((tm, d), lambda i: (i, 0)), pl.BlockSpec((1, d), lambda i: (0, 0))]
    in_specs += [pl.BlockSpec(w.shape, lambda i: (0, 0)) for w in ws]
    return pl.pallas_call(
        functools.partial(_norm_proj_kernel, specs=tuple(specs), parts=parts),
        grid=(m // tm,),
        in_specs=in_specs,
        out_specs=out_specs,
        out_shape=out_shapes,
        compiler_params=_params("parallel"),
        name="norm_proj",
    )(x, g, *ws)


def _mix_mlp_kernel(h_ref, x_ref, wo_ref, gmix_ref, gpre_ref, wup_ref, wdn_ref, gpost_ref, o_ref, *, tf, parts):
    rows = h_ref.shape[0] // parts
    rs = [slice(r * rows, (r + 1) * rows) for r in range(parts)]
    x1 = [x_ref[s, :] + _rms(jnp.dot(h_ref[s, :], wo_ref[...], preferred_element_type=F32), gmix_ref[...])
          for s in rs]
    xn = [_rms(x, gpre_ref[...]).astype(BF16) for x in x1]
    acc = [None] * parts
    for c in range(wup_ref.shape[1] // tf):
        for r in range(parts):
            up = jnp.maximum(jnp.dot(xn[r], wup_ref[:, c * tf:(c + 1) * tf], preferred_element_type=F32), 0.0)
            part = jnp.dot((up * up).astype(BF16), wdn_ref[c * tf:(c + 1) * tf, :], preferred_element_type=F32)
            acc[r] = part if acc[r] is None else acc[r] + part
    for r in range(parts):
        o_ref[rs[r], :] = x1[r] + _rms(acc[r], gpost_ref[...])


def _mix_mlp(h, x, w_o, g_mix, g_pre, w_up, w_down, g_post, tm, tf):
    m, d = x.shape
    k = h.shape[1]
    ff = w_up.shape[1]
    assert m % tm == 0 and ff % tf == 0 and (tm <= ROW_TILE or tm % ROW_TILE == 0)
    resident = lambda shape: pl.BlockSpec(shape, lambda i: (0, 0), pipeline_mode=pl.Buffered(1))
    return pl.pallas_call(
        functools.partial(_mix_mlp_kernel, tf=tf, parts=max(1, tm // ROW_TILE)),
        grid=(m // tm,),
        in_specs=[pl.BlockSpec((tm, k), lambda i: (i, 0)),
                  pl.BlockSpec((tm, d), lambda i: (i, 0)),
                  resident((k, d)),
                  resident((1, d)),
                  resident((1, d)),
                  resident((d, ff)),
                  resident((ff, d)),
                  resident((1, d))],
        out_specs=pl.BlockSpec((tm, d), lambda i: (i, 0)),
        out_shape=jax.ShapeDtypeStruct((m, d), F32),
        compiler_params=_params("parallel", vmem_limit_bytes=DENSE_VMEM_LIMIT_BYTES),
        name="mix_mlp",
    )(h, x, w_o, g_mix, g_pre, w_up, w_down, g_post)


def _mlstm_kernel(q_ref, k_ref, v_ref, o_ref, gate_ref, brow_ref, bcol_ref, gh_ref, c0_ref, n0_ref, m0_ref,
                  hg_ref, c_ref, n_ref, m_ref, *, nb, blk, sub, heads, dk, dv):
    t = pl.program_id(1)

    @pl.when(t == 0)
    def _():
        c_ref[...] = c0_ref[...]
        n_ref[...] = n0_ref[...]
        m_ref[...] = m0_ref[...]

    row = lax.broadcasted_iota(jnp.int32, (blk, blk), 0)
    col = lax.broadcasted_iota(jnp.int32, (blk, blk), 1)
    causal = row >= col
    tril = causal.astype(F32)
    triu = (row <= col).astype(F32)

    def square(a):
        if a.shape[1] < LANES:
            a = jnp.concatenate([a, jnp.zeros((a.shape[0], LANES - a.shape[1]), F32)], axis=1)
        if a.shape[0] < LANES:
            a = jnp.concatenate([a, jnp.zeros((LANES - a.shape[0], a.shape[1]), F32)], axis=0)
        return a

    all_pairs = [(b, h) for b in range(nb) for h in range(heads)]

    def one_block(rows):
        for g0 in range(0, len(all_pairs), MLSTM_HEAD_GROUP):
            _mlstm_group(all_pairs[g0:g0 + MLSTM_HEAD_GROUP], rows, q_ref, k_ref, v_ref, o_ref, gate_ref, brow_ref,
                         bcol_ref, gh_ref, hg_ref, c_ref, n_ref, m_ref, causal, tril, triu, square,
                         blk=blk, heads=heads, dk=dk, dv=dv)

    if sub == 1:
        one_block(slice(None))
    else:
        def body(j, carry):
            one_block(pl.ds(pl.multiple_of(j * blk, blk), blk))
            return carry

        lax.fori_loop(0, sub, body, 0)


def _mlstm_group(pairs, rows, q_ref, k_ref, v_ref, o_ref, gate_ref, brow_ref, bcol_ref, gh_ref, hg_ref, c_ref, n_ref,
                 m_ref, causal, tril, triu, square, *, blk, heads, dk, dv):
    hr = range(len(pairs))
    streams = sorted({b for b, _ in pairs})
    qs = [q_ref[b, rows, h * dk:(h + 1) * dk] for b, h in pairs]
    ks = [k_ref[b, rows, h * dk:(h + 1) * dk] for b, h in pairs]
    vs = [v_ref[b, rows, h * dv:(h + 1) * dv] for b, h in pairs]
    c0s = [c_ref[b, h] for b, h in pairs]
    n0s = [n_ref[b, h] for b, h in pairs]
    m0s = [m_ref[b, h:h + 1, :] for b, h in pairs]
    ones = jnp.ones((blk, LANES), BF16)
    v1s = [jnp.concatenate([vs[h], ones], axis=1) for h in hr]
    cns = [jnp.concatenate([c0s[h], n0s[h]], axis=1).astype(BF16) for h in hr]

    qk = [_dot_nt(qs[h], ks[h]) for h in hr]
    qcn = [jnp.dot(qs[h], cns[h], preferred_element_type=F32) for h in hr]

    z_cols, fcum_cols, r_rows = {}, {}, {}
    for b in streams:
        gates = gate_ref[b, rows, :]
        z_cols[b] = gates + brow_ref[...]
        fcum_cols[b] = jnp.dot(tril, _log_sigmoid(z_cols[b]), preferred_element_type=F32,
                               precision=lax.Precision.HIGHEST)
        gates_row = square(gates).T[:, :blk]
        z_row = gates_row[0:2 * heads, :] + bcol_ref[...]
        lf_row = _log_sigmoid(z_row[heads:2 * heads, :])
        fcum_row = jnp.dot(lf_row, triu, preferred_element_type=F32, precision=lax.Precision.HIGHEST)
        r_rows[b] = z_row[0:heads, :] - fcum_row
    r_row = [r_rows[b][h:h + 1, :] for b, h in pairs]
    fcs = [jnp.broadcast_to(fcum_cols[b][:, heads + h:heads + h + 1], (blk, LANES)) for b, h in pairs]
    lis = [jnp.broadcast_to(z_cols[b][:, h:h + 1], (blk, LANES)) for b, h in pairs]

    def wide(x):
        return x[:, :blk] if blk <= LANES else jnp.concatenate([x] * (blk // LANES), axis=1)

    dmat = [jnp.where(causal, wide(fcs[h]) + r_row[h], NEG_INF) for h in hr]
    a = [fcs[h] + m0s[h] for h in hr]
    m_new = [jnp.maximum(a[h], jnp.max(dmat[h], axis=1, keepdims=True)) for h in hr]
    s = [qk[h] * jnp.exp(dmat[h] - wide(m_new[h])) for h in hr]
    inter = [jnp.exp(a[h] - m_new[h]) for h in hr]
    sv1 = [jnp.dot(s[h].astype(BF16), v1s[h], preferred_element_type=F32) for h in hr]
    qn = [sv1[h][:, dv:] + inter[h] * qcn[h][:, dv:] for h in hr]
    hh = [(sv1[h][:, :dv] + inter[h] * qcn[h][:, :dv]) / jnp.maximum(jnp.abs(qn[h]), jnp.exp(-m_new[h]))
          for h in hr]

    for i, (b, h) in enumerate(pairs):
        hn = _rms(hh[i], gh_ref[:, h * dv:(h + 1) * dv])
        og = o_ref[b, rows, h * dv:(h + 1) * dv].astype(F32)
        hg_ref[b, rows, h * dv:(h + 1) * dv] = (hn * jax.nn.sigmoid(og)).astype(BF16)

    m_last = [m_new[h][blk - 1:blk, :] for h in hr]
    w_last = [jnp.exp(fcs[h][blk - 1:blk, :] - fcs[h] + lis[h] - m_last[h]) for h in hr]
    decay = [jnp.exp(a[h][blk - 1:blk, :] - m_last[h]) for h in hr]
    kw = [ks[h].astype(F32) * w_last[h][:, :dk] for h in hr]
    kv1 = [_dot_tn(kw[h].astype(BF16), v1s[h]) for h in hr]
    for i, (b, h) in enumerate(pairs):
        c_ref[b, h] = decay[i] * c0s[i] + kv1[i][:, :dv]
        n_ref[b, h] = decay[i] * n0s[i] + kv1[i][:, dv:]
        m_ref[b, h:h + 1, :] = m_last[i]


def _mlstm(q, k, v, o, gates, b_row, b_col, g_h, c0, n0, m0, blk, nb, sub):
    bsz, seqlen, _ = q.shape
    heads, dk, dv = c0.shape[1], c0.shape[2], c0.shape[3]
    step = blk * sub
    assert seqlen % step == 0 and bsz % nb == 0 and dv == LANES
    row_map = lambda b, t: (b, t, 0)
    const2 = lambda b, t: (0, 0)
    st4 = lambda b, t: (b, 0, 0, 0)
    st3 = lambda b, t: (b, 0, 0)
    return pl.pallas_call(
        functools.partial(_mlstm_kernel, nb=nb, blk=blk, sub=sub, heads=heads, dk=dk, dv=dv),
        grid=(bsz // nb, seqlen // step),
        in_specs=[pl.BlockSpec((nb, step, heads * dk), row_map),
                  pl.BlockSpec((nb, step, heads * dk), row_map),
                  pl.BlockSpec((nb, step, heads * dv), row_map),
                  pl.BlockSpec((nb, step, heads * dv), row_map),
                  pl.BlockSpec((nb, step, LANES), row_map),
                  pl.BlockSpec((1, LANES), const2),
                  pl.BlockSpec((2 * heads, 1), const2),
                  pl.BlockSpec((1, heads * dv), const2),
                  pl.BlockSpec((nb, heads, dk, dv), st4),
                  pl.BlockSpec((nb, heads, dk, LANES), st4),
                  pl.BlockSpec((nb, heads, LANES), st3)],
        out_specs=[pl.BlockSpec((nb, step, heads * dv), row_map),
                   pl.BlockSpec((nb, heads, dk, dv), st4),
                   pl.BlockSpec((nb, heads, dk, LANES), st4),
                   pl.BlockSpec((nb, heads, LANES), st3)],
        out_shape=[jax.ShapeDtypeStruct((bsz, seqlen, heads * dv), BF16),
                   jax.ShapeDtypeStruct((bsz, heads, dk, dv), F32),
                   jax.ShapeDtypeStruct((bsz, heads, dk, LANES), F32),
                   jax.ShapeDtypeStruct((bsz, heads, LANES), F32)],
        compiler_params=_params("parallel", "arbitrary"),
        name="mlstm",
    )(q, k, v, o, gates, b_row, b_col, g_h, c0, n0, m0)


def _stack_maps(q):
    lane = lax.broadcasted_iota(jnp.int32, q.shape, 1)
    zero = jnp.zeros_like(q)
    return jnp.concatenate([jnp.where(lane < HD_B, q, zero), jnp.where(lane >= HD_B, q, zero)], axis=0)


def _lambda(lam_ref, lam_init):
    lv = lam_ref[...]
    a = jnp.sum(lv[0:1] * lv[1:2], axis=1, keepdims=True)
    b = jnp.sum(lv[2:3] * lv[3:4], axis=1, keepdims=True)
    return jnp.exp(a) - jnp.exp(b) + lam_init


def _finish_head(acc, l, lam, g_sub, lam_init, tq):
    w = acc / l
    o = w[0:tq] - lam * w[tq:2 * tq]
    return _rms(o, g_sub) * (1.0 - lam_init)


def _attn_prompt_kernel(q_ref, kt_ref, v_ref, lam_ref, gsub_ref, o_ref, qs_sc, m_sc, l_sc, acc_sc,
                        *, tile, hg, qt, lam_init):
    first = pl.program_id(2) * qt
    lam = _lambda(lam_ref, lam_init)

    def one_query_tile(t, carry):
        _attn_query_tile(first + t, pl.ds(pl.multiple_of(t * tile, tile), tile), q_ref, kt_ref, v_ref, gsub_ref, o_ref,
                         qs_sc, m_sc, l_sc, acc_sc, lam, tile=tile, hg=hg, lam_init=lam_init)
        return carry

    lax.fori_loop(0, qt, one_query_tile, 0)


def _attn_query_tile(i, rows, q_ref, kt_ref, v_ref, gsub_ref, o_ref, qs_sc, m_sc, l_sc, acc_sc, lam,
                     *, tile, hg, lam_init):
    heads = range(hg)
    for g in heads:
        qs_sc[g] = _stack_maps(q_ref[0, rows, g * LANES:(g + 1) * LANES])
    m_sc[...] = jnp.full(m_sc.shape, NEG_INF, F32)
    l_sc[...] = jnp.zeros_like(l_sc)
    acc_sc[...] = jnp.zeros_like(acc_sc)
    n_lane_tiles = tile // LANES

    def step(j, masked):
        ss = [jnp.dot(qs_sc[g], kt_ref[0, j, g * LANES:(g + 1) * LANES, :], preferred_element_type=F32).astype(BF16)
              for g in heads]
        if masked:
            row = lax.broadcasted_iota(jnp.int32, ss[0].shape, 0) & (tile - 1)
            col = lax.broadcasted_iota(jnp.int32, ss[0].shape, 1)
            ss = [jnp.where((col // CHUNK) <= (row // CHUNK), s, NEG_INF).astype(BF16) for s in ss]
        start = pl.multiple_of(j * tile, tile)
        m_prev, m_new, alpha, ps, pv = [], [], [], [], []
        for g in heads:
            m_prev.append(m_sc[g])
            m_new.append(jnp.maximum(m_prev[g], jnp.max(ss[g], axis=1, keepdims=True).astype(F32)))
            alpha.append(jnp.exp2(m_prev[g] - m_new[g]))
            ps.append(jnp.exp2(ss[g] - jnp.concatenate([m_new[g].astype(BF16)] * n_lane_tiles, axis=1)))
            pv.append(jnp.dot(ps[g], v_ref[0, g, pl.ds(start, tile), :], preferred_element_type=F32))
        for g in heads:
            p_sum = ps[g][:, 0:LANES]
            for c in range(1, n_lane_tiles):
                p_sum = p_sum + ps[g][:, c * LANES:(c + 1) * LANES]
            l_sc[g] = alpha[g] * l_sc[g] + p_sum.astype(F32)
            acc_sc[g] = alpha[g] * acc_sc[g] + pv[g]
            m_sc[g] = m_new[g]

    def body(jj, carry):
        for u in range(KEY_LOOP_UNROLL):
            step(KEY_LOOP_UNROLL * jj + u, False)
        return carry

    n_full = i // KEY_LOOP_UNROLL
    lax.fori_loop(0, n_full, body, 0)
    for rem in range(1, KEY_LOOP_UNROLL):
        @pl.when(i - KEY_LOOP_UNROLL * n_full >= rem)
        def _():
            step(KEY_LOOP_UNROLL * n_full + rem - 1, False)
    step(i, True)
    for g in heads:
        l = jnp.sum(l_sc[g], axis=1, keepdims=True)
        o = _finish_head(acc_sc[g], l, lam, gsub_ref[:, g * LANES:(g + 1) * LANES], lam_init, tile)
        o_ref[0, rows, g * LANES:(g + 1) * LANES] = o.astype(BF16)


def _attn_prompt(q, kt, v, lam_vec, g_sub, lam_init, hg, qt):
    bsz, seqlen, width = q.shape
    tile = kt.shape[3]
    heads = width // LANES
    gw = hg * LANES
    assert seqlen % (tile * qt) == 0 and tile % CHUNK == 0 and (tile & (tile - 1)) == 0 and tile % LANES == 0
    assert heads % hg == 0
    return pl.pallas_call(
        functools.partial(_attn_prompt_kernel, tile=tile, hg=hg, qt=qt, lam_init=lam_init),
        grid=(bsz, heads // hg, seqlen // (tile * qt)),
        in_specs=[pl.BlockSpec((1, tile * qt, gw), lambda b, h, i: (b, i, h)),
                  pl.BlockSpec((1, seqlen // tile, gw, tile), lambda b, h, i: (b, 0, h, 0)),
                  pl.BlockSpec((1, hg, seqlen, LANES), lambda b, h, i: (b, h, 0, 0)),
                  pl.BlockSpec((4, HD_B), lambda b, h, i: (0, 0)),
                  pl.BlockSpec((1, gw), lambda b, h, i: (0, h))],
        out_specs=pl.BlockSpec((1, tile * qt, gw), lambda b, h, i: (b, i, h)),
        out_shape=jax.ShapeDtypeStruct((bsz, seqlen, width), BF16),
        scratch_shapes=[pltpu.VMEM((hg, 2 * tile, LANES), BF16),
                        pltpu.VMEM((hg, 2 * tile, LANES), F32),
                        pltpu.VMEM((hg, 2 * tile, LANES), F32),
                        pltpu.VMEM((hg, 2 * tile, LANES), F32)],
        compiler_params=_params("parallel", "parallel", "arbitrary"),
        name="attn_prompt",
    )(q, kt, v, lam_vec, g_sub)


def _kv_proj_t_kernel(x_ref, g_ref, gq_ref, wkt_ref, wv_ref, wq_ref, kt_ref, ktb_ref, v_ref, vb_ref, q_ref,
                      *, heads, q_scale, parts):
    rows = x_ref.shape[0] // parts
    rs = [slice(r * rows, (r + 1) * rows) for r in range(parts)]
    xs = [x_ref[s, :] for s in rs]
    xr = [x * lax.rsqrt(jnp.mean(x * x, axis=-1, keepdims=True) + EPS) for x in xs]
    xq = [(x * gq_ref[...]).astype(BF16) for x in xr]
    xn = [(x * g_ref[...]).astype(BF16) for x in xr]
    for r in range(parts):
        q = jnp.dot(xq[r], wq_ref[...], preferred_element_type=F32)
        q_ref[rs[r], :] = (q * q_scale).astype(BF16)
    for r in range(parts):
        kt = _dot_nt(wkt_ref[...], xn[r])
        kt_ref[0, :, rs[r]] = kt
        ktb_ref[0, r] = kt.astype(BF16)
    for r in range(parts):
        v = jnp.dot(xn[r], wv_ref[...], preferred_element_type=F32)
        v_ref[rs[r], :] = v
        for h in range(heads):
            vb_ref[0, h, rs[r], :] = v[:, h * LANES:(h + 1) * LANES].astype(BF16)


def _kv_proj_t(x, g_kv, g_q, wkt, wv, wq, q_scale, bsz, tm, kt_tile):
    rows, d = x.shape
    seqlen = rows // bsz
    width = wv.shape[1]
    heads = width // LANES
    assert seqlen % tm == 0 and tm % kt_tile == 0
    nt = seqlen // tm
    parts = tm // kt_tile
    row_blk = lambda b, i: (b * nt + i, 0)
    resident = lambda shape: pl.BlockSpec(shape, lambda b, i: (0, 0), pipeline_mode=pl.Buffered(1))
    return pl.pallas_call(
        functools.partial(_kv_proj_t_kernel, heads=heads, q_scale=q_scale, parts=parts),
        grid=(bsz, nt),
        in_specs=[pl.BlockSpec((tm, d), row_blk),
                  resident((1, d)),
                  resident((1, d)),
                  resident((width, d)),
                  resident((d, width)),
                  resident((d, width))],
        out_specs=[pl.BlockSpec((1, width, tm), lambda b, i: (b, 0, i)),
                   pl.BlockSpec((1, parts, width, kt_tile), lambda b, i: (b, i, 0, 0)),
                   pl.BlockSpec((tm, width), row_blk),
                   pl.BlockSpec((1, heads, tm, LANES), lambda b, i: (b, 0, i, 0)),
                   pl.BlockSpec((tm, width), row_blk)],
        out_shape=[jax.ShapeDtypeStruct((bsz, width, seqlen), F32),
                   jax.ShapeDtypeStruct((bsz, seqlen // kt_tile, width, kt_tile), BF16),
                   jax.ShapeDtypeStruct((rows, width), F32),
                   jax.ShapeDtypeStruct((bsz, heads, seqlen, LANES), BF16),
                   jax.ShapeDtypeStruct((rows, width), BF16)],
        compiler_params=_params("parallel", "parallel", vmem_limit_bytes=DENSE_VMEM_LIMIT_BYTES),
        name="kv_proj_t",
    )(x, g_kv, g_q, wkt, wv, wq)


def _attn_sample_kernel(q_ref, ckt_ref, cv_ref, kn_ref, vn_ref, lam_ref, gsub_ref, o_ref, qs_sc, m_sc, l_sc, acc_sc,
                        *, lq, heads, past, lam_init):
    j = pl.program_id(1)

    @pl.when(j == 0)
    def _():
        for h in range(heads):
            qs_sc[h] = _stack_maps(q_ref[0, :, h * LANES:(h + 1) * LANES])
        m_sc[...] = jnp.full(m_sc.shape, NEG_INF, F32)
        l_sc[...] = jnp.zeros_like(l_sc)
        acc_sc[...] = jnp.zeros_like(acc_sc)

    hr = range(heads)

    def update(ss, vts):
        m_prev = [m_sc[h] for h in hr]
        m_new = [jnp.maximum(m_prev[h], jnp.max(ss[h], axis=1, keepdims=True)) for h in hr]
        alpha = [jnp.exp2(m_prev[h] - m_new[h]) for h in hr]
        ps = [jnp.exp2(ss[h] - m_new[h]) for h in hr]
        pv = [jnp.dot(ps[h].astype(BF16), vts[h], preferred_element_type=F32) for h in hr]
        for h in hr:
            l_sc[h] = alpha[h] * l_sc[h] + jnp.sum(ps[h], axis=1, keepdims=True)
            acc_sc[h] = alpha[h] * acc_sc[h] + pv[h]
            m_sc[h] = m_new[h]

    v_heads = pltpu.einshape("phd->hpd", cv_ref[0])
    update([jnp.dot(qs_sc[h], ckt_ref[0, h * LANES:(h + 1) * LANES, :].astype(BF16), preferred_element_type=F32)
            for h in hr],
           [v_heads[h].astype(BF16) for h in hr])

    @pl.when(j == pl.num_programs(1) - 1)
    def _():
        row = lax.broadcasted_iota(jnp.int32, (2 * lq, lq), 0)
        row = jnp.where(row >= lq, row - lq, row)
        col = lax.broadcasted_iota(jnp.int32, (2 * lq, lq), 1)
        mask = ((past + col) // CHUNK) <= ((past + row) // CHUNK)
        lam = _lambda(lam_ref, lam_init)
        hs = [slice(h * LANES, (h + 1) * LANES) for h in hr]
        update([jnp.where(mask, _dot_nt(qs_sc[h], kn_ref[0, :, hs[h]]), NEG_INF) for h in hr],
               [vn_ref[0, :, hs[h]] for h in hr])
        for h in hr:
            o = _finish_head(acc_sc[h], l_sc[h], lam, gsub_ref[:, hs[h]], lam_init, lq)
            o_ref[0, :, hs[h]] = o.astype(BF16)


def _attn_sample(q, cache_kt, cache_v, k_new, v_new, lam_vec, g_sub, lam_init, tkc):
    bsz, lq, width = q.shape
    past = cache_kt.shape[2]
    heads = width // LANES
    assert past % tkc == 0 and past % CHUNK == 0
    new_blk = pl.BlockSpec((1, lq, width), lambda b, j: (b, 0, 0))
    return pl.pallas_call(
        functools.partial(_attn_sample_kernel, lq=lq, heads=heads, past=past, lam_init=lam_init),
        grid=(bsz, past // tkc),
        in_specs=[new_blk,
                  pl.BlockSpec((1, width, tkc), lambda b, j: (b, 0, j)),
                  pl.BlockSpec((1, tkc, heads, LANES), lambda b, j: (b, j, 0, 0)),
                  new_blk,
                  new_blk,
                  pl.BlockSpec((4, HD_B), lambda b, j: (0, 0)),
                  pl.BlockSpec((1, width), lambda b, j: (0, 0))],
        out_specs=new_blk,
        out_shape=jax.ShapeDtypeStruct((bsz, lq, width), BF16),
        scratch_shapes=[pltpu.VMEM((heads, 2 * lq, LANES), BF16),
                        pltpu.VMEM((heads, 2 * lq, 1), F32),
                        pltpu.VMEM((heads, 2 * lq, 1), F32),
                        pltpu.VMEM((heads, 2 * lq, LANES), F32)],
        compiler_params=_params("parallel", "arbitrary"),
        name="attn_sample",
    )(q, cache_kt, cache_v, k_new, v_new, lam_vec, g_sub)


def _lambda_init(layer_idx):
    return 0.8 - 0.6 * math.exp(-0.3 * layer_idx)


def _trunk(x3, state, cache, p, mlstm_blk):
    bsz, seqlen, d = x3.shape
    rows = bsz * seqlen
    tm = min(ROW_TILE, rows)
    x = x3.reshape(rows, d)
    row = lambda g: g.reshape(1, -1)

    qk_w = H_A * p["dk"]
    v_w = H_A * p["dv"]
    w_in = p["w_in_a"]
    w_parts = [w_in[:, 0:qk_w], w_in[:, qk_w:2 * qk_w], w_in[:, 2 * qk_w:2 * qk_w + v_w],
               w_in[:, 2 * qk_w + v_w:2 * qk_w + 2 * v_w]]
    w_gate = jnp.pad(w_in[:, 2 * qk_w + 2 * v_w:], ((0, 0), (0, LANES - 2 * H_A)))
    ws = [w.astype(BF16) for w in w_parts] + [w_gate.astype(BF16)]
    specs = [(1.0, (BF16,)), (p["dk"] ** -0.5, (BF16,)), (1.0, (BF16,)), (1.0, (BF16,)), (1.0, (F32,))]
    q, k, v, o, gates = _norm_proj(x, row(p["g_mix_pre"][0]), ws, specs, min(DENSE_ROW_TILE, rows))

    bias = jnp.concatenate([p["b_i_a"], p["b_f_a"]]).astype(F32)
    b_row = jnp.pad(bias, (0, LANES - 2 * H_A)).reshape(1, LANES)
    b_col = bias.reshape(2 * H_A, 1)
    c0, n0, m0 = state
    m0b = jnp.broadcast_to(m0[:, :, None], (bsz, H_A, LANES))
    n0b = jnp.broadcast_to(n0[:, :, :, None], n0.shape + (LANES,))
    r3 = lambda a: a.reshape(bsz, seqlen, a.shape[-1])
    hg, c1, n1b, m1 = _mlstm(r3(q), r3(k), r3(v), r3(o), r3(gates), b_row, b_col, row(p["g_h_a"]), c0, n0b, m0b,
                             mlstm_blk, MLSTM_STREAMS, min(MLSTM_BLOCKS_PER_STEP, seqlen // mlstm_blk))
    n1 = n1b[:, :, :, 0]
    x = _mix_mlp(hg.reshape(rows, v_w), x, p["w_out_a"].astype(BF16), row(p["g_mix_post"][0]),
                 row(p["g_ffn_pre"][0]), p["w_up"][0].astype(BF16), p["w_down"][0].astype(BF16),
                 row(p["g_ffn_post"][0]), min(DENSE_ROW_TILE, rows), FF_TILE)

    kb_w = H_B * 2 * HD_B
    w_k = p["w_kv"][:, :kb_w]
    w_v = p["w_kv"][:, kb_w:].astype(BF16)
    q_scale = HD_B ** -0.5 * math.log2(math.e)
    w_q = p["w_q_b"].astype(BF16)
    g_q = row(p["g_mix_pre"][1])

    lam_init = _lambda_init(1)
    lam_vec = jnp.stack([p["lambda_q1"], p["lambda_k1"], p["lambda_q2"], p["lambda_k2"]]).astype(F32)
    g_sub = row(p["g_sub_b"])
    shape3 = (bsz, seqlen, kb_w)
    if cache is None:
        kt_new, kt_bf, v_new, v_bf, qb = _kv_proj_t(x, row(p["g_kv"]), g_q, w_k.T.astype(BF16), w_v, w_q, q_scale,
                                                    bsz, min(DENSE_ROW_TILE, seqlen), ATTN_TILE)
        k_out = kt_new.reshape(bsz, H_B, 2, HD_B, seqlen).transpose(0, 4, 1, 2, 3)
        att = _attn_prompt(qb.reshape(shape3), kt_bf, v_bf, lam_vec, g_sub, lam_init, ATTN_HEAD_GROUP,
                           min(ATTN_QUERY_TILES_PER_STEP, seqlen // ATTN_TILE))
    else:
        (qb,) = _norm_proj(x, g_q, [w_q], [(q_scale, (BF16,))], tm)
        k_new, k_bf, v_new, v_bf = _norm_proj(x, row(p["g_kv"]), [w_k.astype(BF16), w_v],
                                              [(1.0, (F32, BF16)), (1.0, (F32, BF16))], tm)
        k_out = k_new.reshape(bsz, seqlen, H_B, 2, HD_B)
        ck, cv = cache
        past = ck.shape[1]
        ckt = ck.transpose(0, 2, 3, 4, 1).reshape(bsz, kb_w, past)
        att = _attn_sample(qb.reshape(shape3), ckt, cv, k_bf.reshape(shape3), v_bf.reshape(shape3),
                           lam_vec, g_sub, lam_init, min(SAMPLE_KEY_TILE, past))
    x = _mix_mlp(att.reshape(rows, kb_w), x, p["w_o_b"].astype(BF16), row(p["g_mix_post"][1]),
                 row(p["g_ffn_pre"][1]), p["w_up"][1].astype(BF16), p["w_down"][1].astype(BF16),
                 row(p["g_ffn_post"][1]), min(DENSE_ROW_TILE, rows), FF_TILE)

    y = x.reshape(bsz, seqlen, d)
    v_out = v_new.reshape(bsz, seqlen, H_B, 2 * HD_B)
    return y, (c1[None], n1[None], m1[None, :, :, 0]), k_out, v_out


def kernel(x_prompt, x_sample, cache_k, cache_v, state_c, state_n, state_m, w_in_a, b_i_a, b_f_a, g_h_a, w_out_a,
           g_kv, w_kv, w_q_b, lambda_q1, lambda_k1, lambda_q2, lambda_k2, g_sub_b, w_o_b, g_mix_pre, g_mix_post,
           g_ffn_pre, g_ffn_post, w_up, w_down):
    assert w_in_a.shape[0] == 1 and w_q_b.shape[0] == 1 and g_mix_pre.shape[0] == 2
    dk, dv = state_c.shape[3], state_c.shape[4]
    p = {"dk": dk, "dv": dv,
         "w_in_a": w_in_a[0], "b_i_a": b_i_a[0], "b_f_a": b_f_a[0], "g_h_a": g_h_a[0], "w_out_a": w_out_a[0],
         "g_kv": g_kv, "w_kv": w_kv, "w_q_b": w_q_b[0],
         "lambda_q1": lambda_q1[0], "lambda_k1": lambda_k1[0], "lambda_q2": lambda_q2[0], "lambda_k2": lambda_k2[0],
         "g_sub_b": g_sub_b[0], "w_o_b": w_o_b[0], "g_mix_pre": g_mix_pre, "g_mix_post": g_mix_post,
         "g_ffn_pre": g_ffn_pre, "g_ffn_post": g_ffn_post, "w_up": w_up, "w_down": w_down}
    bsz = x_prompt.shape[0]
    zero_state = (jnp.zeros((bsz, H_A, dk, dv), F32), jnp.zeros((bsz, H_A, dk), F32), jnp.zeros((bsz, H_A), F32))
    y_p, (c_p, n_p, m_p), k_p, v_p = _trunk(x_prompt, zero_state, None, p, MLSTM_BLOCK)
    y_s, (c_s, n_s, m_s), k_s, v_s = _trunk(
        x_sample, (state_c[0].astype(F32), state_n[0].astype(F32), state_m[0].astype(F32)),
        (cache_k, cache_v), p, x_sample.shape[1])
    return (y_p, y_s, c_p, n_p, m_p, k_p, v_p, c_s, n_s, m_s, k_s, v_s)
```

```python
import functools
import math

import jax
import jax.numpy as jnp
from jax import lax
from jax.experimental import pallas as pl
from jax.experimental.pallas import tpu as pltpu

F32 = jnp.float32
BF16 = jnp.bfloat16

EPS = 1e-6
NEG_INF = -1e30
CHUNK = 64
LANES = 128
H_A = 8
H_B = 8
HD_B = 64
MLSTM_BLOCK = 128
MLSTM_HEAD_GROUP = 8
MLSTM_STREAMS = 1
MLSTM_BLOCKS_PER_STEP = 4
ATTN_TILE = 512
KEY_LOOP_UNROLL = 2
ATTN_HEAD_GROUP = 2
ATTN_QUERY_TILES_PER_STEP = 4
ROW_TILE = 512
DENSE_ROW_TILE = 1024
FF_TILE = 1024
SAMPLE_KEY_TILE = 1024
VMEM_LIMIT_BYTES = 48 * 1024 * 1024
DENSE_VMEM_LIMIT_BYTES = 58 * 1024 * 1024


def _params(*sem, vmem_limit_bytes=VMEM_LIMIT_BYTES):
    return pltpu.CompilerParams(dimension_semantics=sem, vmem_limit_bytes=vmem_limit_bytes)


def _rms(x, g):
    return x * lax.rsqrt(jnp.mean(x * x, axis=-1, keepdims=True) + EPS) * g


def _log_sigmoid(x):
    return jnp.minimum(x, 0.0) - jnp.log1p(jnp.exp(-jnp.abs(x)))


def _dot_nt(a, b):
    return lax.dot_general(a, b, (((1,), (1,)), ((), ())), preferred_element_type=F32)


def _dot_tn(a, b):
    return lax.dot_general(a, b, (((0,), (0,)), ((), ())), preferred_element_type=F32)


def _norm_proj_kernel(x_ref, g_ref, *refs, specs, parts):
    n_w = len(specs)
    w_refs, o_refs = refs[:n_w], refs[n_w:]
    rows = x_ref.shape[0] // parts
    rs = [slice(r * rows, (r + 1) * rows) for r in range(parts)]
    xn = [_rms(x_ref[s, :], g_ref[...]).astype(BF16) for s in rs]
    oi = 0
    for w_ref, (scale, dtypes) in zip(w_refs, specs):
        for r in range(parts):
            acc = jnp.dot(xn[r], w_ref[...], preferred_element_type=F32)
            if scale != 1.0:
                acc = acc * scale
            for k, dt in enumerate(dtypes):
                o_refs[oi + k][rs[r], :] = acc.astype(dt)
        oi += len(dtypes)


def _norm_proj(x, g, ws, specs, tm):
    m, d = x.shape
    assert m % tm == 0 and (tm <= ROW_TILE or tm % ROW_TILE == 0)
    parts = max(1, tm // ROW_TILE)
    out_shapes, out_specs = [], []
    for w, (_, dtypes) in zip(ws, specs):
        for dt in dtypes:
            out_shapes.append(jax.ShapeDtypeStruct((m, w.shape[1]), dt))
            out_specs.append(pl.BlockSpec((tm, w.shape[1]), lambda i: (i, 0)))
    in_specs = [pl.BlockSpec((tm, d), lambda i: (i, 0)), pl.BlockSpec((1, d), lambda i: (0, 0))]
    in_specs += [pl.BlockSpec(w.shape, lambda i: (0, 0)) for w in ws]
    return pl.pallas_call(
        functools.partial(_norm_proj_kernel, specs=tuple(specs), parts=parts),
        grid=(m // tm,),
        in_specs=in_specs,
        out_specs=out_specs,
        out_shape=out_shapes,
        compiler_params=_params("parallel"),
        name="norm_proj",
    )(x, g, *ws)


def _mix_mlp_kernel(h_ref, x_ref, wo_ref, gmix_ref, gpre_ref, wup_ref, wdn_ref, gpost_ref, o_ref, *, tf, parts):
    rows = h_ref.shape[0] // parts
    rs = [slice(r * rows, (r + 1) * rows) for r in range(parts)]
    x1 = [x_ref[s, :] + _rms(jnp.dot(h_ref[s, :], wo_ref[...], preferred_element_type=F32), gmix_ref[...])
          for s in rs]
    xn = [_rms(x, gpre_ref[...]).astype(BF16) for x in x1]
    acc = [None] * parts
    for c in range(wup_ref.shape[1] // tf):
        for r in range(parts):
            up = jnp.maximum(jnp.dot(xn[r], wup_ref[:, c * tf:(c + 1) * tf], preferred_element_type=F32), 0.0)
            part = jnp.dot((up * up).astype(BF16), wdn_ref[c * tf:(c + 1) * tf, :], preferred_element_type=F32)
            acc[r] = part if acc[r] is None else acc[r] + part
    for r in range(parts):
        o_ref[rs[r], :] = x1[r] + _rms(acc[r], gpost_ref[...])


def _mix_mlp(h, x, w_o, g_mix, g_pre, w_up, w_down, g_post, tm, tf):
    m, d = x.shape
    k = h.shape[1]
    ff = w_up.shape[1]
    assert m % tm == 0 and ff % tf == 0 and (tm <= ROW_TILE or tm % ROW_TILE == 0)
    resident = lambda shape: pl.BlockSpec(shape, lambda i: (0, 0), pipeline_mode=pl.Buffered(1))
    return pl.pallas_call(
        functools.partial(_mix_mlp_kernel, tf=tf, parts=max(1, tm // ROW_TILE)),
        grid=(m // tm,),
        in_specs=[pl.BlockSpec((tm, k), lambda i: (i, 0)),
                  pl.BlockSpec((tm, d), lambda i: (i, 0)),
                  resident((k, d)),
                  resident((1, d)),
                  resident((1, d)),
                  resident((d, ff)),
                  resident((ff, d)),
                  resident((1, d))],
        out_specs=pl.BlockSpec((tm, d), lambda i: (i, 0)),
        out_shape=jax.ShapeDtypeStruct((m, d), F32),
        compiler_params=_params("parallel", vmem_limit_bytes=DENSE_VMEM_LIMIT_BYTES),
        name="mix_mlp",
    )(h, x, w_o, g_mix, g_pre, w_up, w_down, g_post)


def _mlstm_kernel(q_ref, k_ref, v_ref, o_ref, gate_ref, brow_ref, bcol_ref, gh_ref, c0_ref, n0_ref, m0_ref,
                  hg_ref, c_ref, n_ref, m_ref, *, nb, blk, sub, heads, dk, dv):
    t = pl.program_id(1)

    @pl.when(t == 0)
    def _():
        c_ref[...] = c0_ref[...]
        n_ref[...] = n0_ref[...]
        m_ref[...] = m0_ref[...]

    row = lax.broadcasted_iota(jnp.int32, (blk, blk), 0)
    col = lax.broadcasted_iota(jnp.int32, (blk, blk), 1)
    causal = row >= col
    tril = causal.astype(F32)
    triu = (row <= col).astype(F32)

    def square(a):
        if a.shape[1] < LANES:
            a = jnp.concatenate([a, jnp.zeros((a.shape[0], LANES - a.shape[1]), F32)], axis=1)
        if a.shape[0] < LANES:
            a = jnp.concatenate([a, jnp.zeros((LANES - a.shape[0], a.shape[1]), F32)], axis=0)
        return a

    all_pairs = [(b, h) for b in range(nb) for h in range(heads)]

    def one_block(rows):
        for g0 in range(0, len(all_pairs), MLSTM_HEAD_GROUP):
            _mlstm_group(all_pairs[g0:g0 + MLSTM_HEAD_GROUP], rows, q_ref, k_ref, v_ref, o_ref, gate_ref, brow_ref,
                         bcol_ref, gh_ref, hg_ref, c_ref, n_ref, m_ref, causal, tril, triu, square,
                         blk=blk, heads=heads, dk=dk, dv=dv)

    if sub == 1:
        one_block(slice(None))
    else:
        def body(j, carry):
            one_block(pl.ds(pl.multiple_of(j * blk, blk), blk))
            return carry

        lax.fori_loop(0, sub, body, 0)


def _mlstm_group(pairs, rows, q_ref, k_ref, v_ref, o_ref, gate_ref, brow_ref, bcol_ref, gh_ref, hg_ref, c_ref, n_ref,
                 m_ref, causal, tril, triu, square, *, blk, heads, dk, dv):
    hr = range(len(pairs))
    streams = sorted({b for b, _ in pairs})
    qs = [q_ref[b, rows, h * dk:(h + 1) * dk] for b, h in pairs]
    ks = [k_ref[b, rows, h * dk:(h + 1) * dk] for b, h in pairs]
    vs = [v_ref[b, rows, h * dv:(h + 1) * dv] for b, h in pairs]
    c0s = [c_ref[b, h] for b, h in pairs]
    n0s = [n_ref[b, h] for b, h in pairs]
    m0s = [m_ref[b, h:h + 1, :] for b, h in pairs]
    ones = jnp.ones((blk, LANES), BF16)
    v1s = [jnp.concatenate([vs[h], ones], axis=1) for h in hr]
    cns = [jnp.concatenate([c0s[h], n0s[h]], axis=1).astype(BF16) for h in hr]

    qk = [_dot_nt(qs[h], ks[h]) for h in hr]
    qcn = [jnp.dot(qs[h], cns[h], preferred_element_type=F32) for h in hr]

    z_cols, fcum_cols, r_rows = {}, {}, {}
    for b in streams:
        gates = gate_ref[b, rows, :]
        z_cols[b] = gates + brow_ref[...]
        fcum_cols[b] = jnp.dot(tril, _log_sigmoid(z_cols[b]), preferred_element_type=F32,
                               precision=lax.Precision.HIGHEST)
        gates_row = square(gates).T[:, :blk]
        z_row = gates_row[0:2 * heads, :] + bcol_ref[...]
        lf_row = _log_sigmoid(z_row[heads:2 * heads, :])
        fcum_row = jnp.dot(lf_row, triu, preferred_element_type=F32, precision=lax.Precision.HIGHEST)
        r_rows[b] = z_row[0:heads, :] - fcum_row
    r_row = [r_rows[b][h:h + 1, :] for b, h in pairs]
    fcs = [jnp.broadcast_to(fcum_cols[b][:, heads + h:heads + h + 1], (blk, LANES)) for b, h in pairs]
    lis = [jnp.broadcast_to(z_cols[b][:, h:h + 1], (blk, LANES)) for b, h in pairs]

    def wide(x):
        return x[:, :blk] if blk <= LANES else jnp.concatenate([x] * (blk // LANES), axis=1)

    dmat = [jnp.where(causal, wide(fcs[h]) + r_row[h], NEG_INF) for h in hr]
    a = [fcs[h] + m0s[h] for h in hr]
    m_new = [jnp.maximum(a[h], jnp.max(dmat[h], axis=1, keepdims=True)) for h in hr]
    s = [qk[h] * jnp.exp(dmat[h] - wide(m_new[h])) for h in hr]
    inter = [jnp.exp(a[h] - m_new[h]) for h in hr]
    sv1 = [jnp.dot(s[h].astype(BF16), v1s[h], preferred_element_type=F32) for h in hr]
    qn = [sv1[h][:, dv:] + inter[h] * qcn[h][:, dv:] for h in hr]
    hh = [(sv1[h][:, :dv] + inter[h] * qcn[h][:, :dv]) / jnp.maximum(jnp.abs(qn[h]), jnp.exp(-m_new[h]))
          for h in hr]

    for i, (b, h) in enumerate(pairs):
        hn = _rms(hh[i], gh_ref[:, h * dv:(h + 1) * dv])
        og = o_ref[b, rows, h * dv:(h + 1) * dv].astype(F32)
        hg_ref[b, rows, h * dv:(h + 1) * dv] = (hn * (0.5 * jnp.tanh(0.5 * og) + 0.5)).astype(BF16)

    m_last = [m_new[h][blk - 1:blk, :] for h in hr]
    w_last = [jnp.exp(fcs[h][blk - 1:blk, :] - fcs[h] + lis[h] - m_last[h]) for h in hr]
    decay = [jnp.exp(a[h][blk - 1:blk, :] - m_last[h]) for h in hr]
    kw = [ks[h].astype(F32) * w_last[h][:, :dk] for h in hr]
    kv1 = [_dot_tn(kw[h].astype(BF16), v1s[h]) for h in hr]
    for i, (b, h) in enumerate(pairs):
        c_ref[b, h] = decay[i] * c0s[i] + kv1[i][:, :dv]
        n_ref[b, h] = decay[i] * n0s[i] + kv1[i][:, dv:]
        m_ref[b, h:h + 1, :] = m_last[i]


def _mlstm(q, k, v, o, gates, b_row, b_col, g_h, c0, n0, m0, blk, nb, sub):
    bsz, seqlen, _ = q.shape
    heads, dk, dv = c0.shape[1], c0.shape[2], c0.shape[3]
    step = blk * sub
    assert seqlen % step == 0 and bsz % nb == 0 and dv == LANES
    row_map = lambda b, t: (b, t, 0)
    const2 = lambda b, t: (0, 0)
    st4 = lambda b, t: (b, 0, 0, 0)
    st3 = lambda b, t: (b, 0, 0)
    return pl.pallas_call(
        functools.partial(_mlstm_kernel, nb=nb, blk=blk, sub=sub, heads=heads, dk=dk, dv=dv),
        grid=(bsz // nb, seqlen // step),
        in_specs=[pl.BlockSpec((nb, step, heads * dk), row_map),
                  pl.BlockSpec((nb, step, heads * dk), row_map),
                  pl.BlockSpec((nb, step, heads * dv), row_map),
                  pl.BlockSpec((nb, step, heads * dv), row_map),
                  pl.BlockSpec((nb, step, LANES), row_map),
                  pl.BlockSpec((1, LANES), const2),
                  pl.BlockSpec((2 * heads, 1), const2),
                  pl.BlockSpec((1, heads * dv), const2),
                  pl.BlockSpec((nb, heads, dk, dv), st4),
                  pl.BlockSpec((nb, heads, dk, LANES), st4),
                  pl.BlockSpec((nb, heads, LANES), st3)],
        out_specs=[pl.BlockSpec((nb, step, heads * dv), row_map),
                   pl.BlockSpec((nb, heads, dk, dv), st4),
                   pl.BlockSpec((nb, heads, dk, LANES), st4),
                   pl.BlockSpec((nb, heads, LANES), st3)],
        out_shape=[jax.ShapeDtypeStruct((bsz, seqlen, heads * dv), BF16),
                   jax.ShapeDtypeStruct((bsz, heads, dk, dv), F32),
                   jax.ShapeDtypeStruct((bsz, heads, dk, LANES), F32),
                   jax.ShapeDtypeStruct((bsz, heads, LANES), F32)],
        compiler_params=_params("parallel", "arbitrary"),
        name="mlstm",
    )(q, k, v, o, gates, b_row, b_col, g_h, c0, n0, m0)


def _stack_maps(q):
    lane = lax.broadcasted_iota(jnp.int32, q.shape, 1)
    zero = jnp.zeros_like(q)
    return jnp.concatenate([jnp.where(lane < HD_B, q, zero), jnp.where(lane >= HD_B, q, zero)], axis=0)


def _lambda(lam_ref, lam_init):
    lv = lam_ref[...]
    a = jnp.sum(lv[0:1] * lv[1:2], axis=1, keepdims=True)
    b = jnp.sum(lv[2:3] * lv[3:4], axis=1, keepdims=True)
    return jnp.exp(a) - jnp.exp(b) + lam_init


def _finish_head(acc, l, lam, g_sub, lam_init, tq):
    w = acc / l
    o = w[0:tq] - lam * w[tq:2 * tq]
    return _rms(o, g_sub) * (1.0 - lam_init)


def _attn_prompt_kernel(q_ref, kt_ref, v_ref, lam_ref, gsub_ref, o_ref, qs_sc, m_sc, l_sc, acc_sc,
                        *, tile, hg, qt, lam_init):
    first = pl.program_id(2) * qt
    lam = _lambda(lam_ref, lam_init)

    def one_query_tile(t, carry):
        _attn_query_tile(first + t, pl.ds(pl.multiple_of(t * tile, tile), tile), q_ref, kt_ref, v_ref, gsub_ref, o_ref,
                         qs_sc, m_sc, l_sc, acc_sc, lam, tile=tile, hg=hg, lam_init=lam_init)
        return carry

    lax.fori_loop(0, qt, one_query_tile, 0)


def _attn_query_tile(i, rows, q_ref, kt_ref, v_ref, gsub_ref, o_ref, qs_sc, m_sc, l_sc, acc_sc, lam,
                     *, tile, hg, lam_init):
    heads = range(hg)
    for g in heads:
        qs_sc[g] = _stack_maps(q_ref[0, rows, g * LANES:(g + 1) * LANES])
    m_sc[...] = jnp.full(m_sc.shape, NEG_INF, F32)
    l_sc[...] = jnp.zeros_like(l_sc)
    acc_sc[...] = jnp.zeros_like(acc_sc)
    n_lane_tiles = tile // LANES

    def step(j, masked):
        ss = [jnp.dot(qs_sc[g], kt_ref[0, j, g * LANES:(g + 1) * LANES, :], preferred_element_type=F32).astype(BF16)
              for g in heads]
        if masked:
            row = lax.broadcasted_iota(jnp.int32, ss[0].shape, 0) & (tile - 1)
            col = lax.broadcasted_iota(jnp.int32, ss[0].shape, 1)
            ss = [jnp.where((col // CHUNK) <= (row // CHUNK), s, NEG_INF).astype(BF16) for s in ss]
        start = pl.multiple_of(j * tile, tile)
        m_prev, m_new, alpha, ps, pv = [], [], [], [], []
        for g in heads:
            m_prev.append(m_sc[g])
            m_new.append(jnp.maximum(m_prev[g], jnp.max(ss[g], axis=1, keepdims=True).astype(F32)))
            alpha.append(jnp.exp2(m_prev[g] - m_new[g]))
            ps.append(jnp.exp2(ss[g] - jnp.concatenate([m_new[g].astype(BF16)] * n_lane_tiles, axis=1)))
            pv.append(jnp.dot(ps[g], v_ref[0, g, pl.ds(start, tile), :], preferred_element_type=F32))
        for g in heads:
            p_sum = ps[g][:, 0:LANES]
            for c in range(1, n_lane_tiles):
                p_sum = p_sum + ps[g][:, c * LANES:(c + 1) * LANES]
            l_sc[g] = alpha[g] * l_sc[g] + p_sum.astype(F32)
            acc_sc[g] = alpha[g] * acc_sc[g] + pv[g]
            m_sc[g] = m_new[g]

    def body(jj, carry):
        for u in range(KEY_LOOP_UNROLL):
            step(KEY_LOOP_UNROLL * jj + u, False)
        return carry

    n_full = i // KEY_LOOP_UNROLL
    lax.fori_loop(0, n_full, body, 0)
    for rem in range(1, KEY_LOOP_UNROLL):
        @pl.when(i - KEY_LOOP_UNROLL * n_full >= rem)
        def _():
            step(KEY_LOOP_UNROLL * n_full + rem - 1, False)
    step(i, True)
    for g in heads:
        l = jnp.sum(l_sc[g], axis=1, keepdims=True)
        o = _finish_head(acc_sc[g], l, lam, gsub_ref[:, g * LANES:(g + 1) * LANES], lam_init, tile)
        o_ref[0, rows, g * LANES:(g + 1) * LANES] = o.astype(BF16)


def _attn_prompt(q, kt, v, lam_vec, g_sub, lam_init, hg, qt):
    bsz, seqlen, width = q.shape
    tile = kt.shape[3]
    heads = width // LANES
    gw = hg * LANES
    assert seqlen % (tile * qt) == 0 and tile % CHUNK == 0 and (tile & (tile - 1)) == 0 and tile % LANES == 0
    assert heads % hg == 0
    return pl.pallas_call(
        functools.partial(_attn_prompt_kernel, tile=tile, hg=hg, qt=qt, lam_init=lam_init),
        grid=(bsz, heads // hg, seqlen // (tile * qt)),
        in_specs=[pl.BlockSpec((1, tile * qt, gw), lambda b, h, i: (b, i, h)),
                  pl.BlockSpec((1, seqlen // tile, gw, tile), lambda b, h, i: (b, 0, h, 0)),
                  pl.BlockSpec((1, hg, seqlen, LANES), lambda b, h, i: (b, h, 0, 0)),
                  pl.BlockSpec((4, HD_B), lambda b, h, i: (0, 0)),
                  pl.BlockSpec((1, gw), lambda b, h, i: (0, h))],
        out_specs=pl.BlockSpec((1, tile * qt, gw), lambda b, h, i: (b, i, h)),
        out_shape=jax.ShapeDtypeStruct((bsz, seqlen, width), BF16),
        scratch_shapes=[pltpu.VMEM((hg, 2 * tile, LANES), BF16),
                        pltpu.VMEM((hg, 2 * tile, LANES), F32),
                        pltpu.VMEM((hg, 2 * tile, LANES), F32),
                        pltpu.VMEM((hg, 2 * tile, LANES), F32)],
        compiler_params=_params("parallel", "parallel", "arbitrary"),
        name="attn_prompt",
    )(q, kt, v, lam_vec, g_sub)


def _kv_proj_t_kernel(x_ref, g_ref, gq_ref, wkt_ref, wv_ref, wq_ref, kt_ref, ktb_ref, v_ref, vb_ref, q_ref,
                      *, heads, q_scale, parts):
    rows = x_ref.shape[0] // parts
    rs = [slice(r * rows, (r + 1) * rows) for r in range(parts)]
    xs = [x_ref[s, :] for s in rs]
    xr = [x * lax.rsqrt(jnp.mean(x * x, axis=-1, keepdims=True) + EPS) for x in xs]
    xq = [(x * gq_ref[...]).astype(BF16) for x in xr]
    xn = [(x * g_ref[...]).astype(BF16) for x in xr]
    for r in range(parts):
        q = jnp.dot(xq[r], wq_ref[...], preferred_element_type=F32)
        q_ref[rs[r], :] = (q * q_scale).astype(BF16)
    for r in range(parts):
        kt = _dot_nt(wkt_ref[...], xn[r])
        kt_ref[0, :, rs[r]] = kt
        ktb_ref[0, r] = kt.astype(BF16)
    for r in range(parts):
        v = jnp.dot(xn[r], wv_ref[...], preferred_element_type=F32)
        v_ref[rs[r], :] = v
        for h in range(heads):
            vb_ref[0, h, rs[r], :] = v[:, h * LANES:(h + 1) * LANES].astype(BF16)


def _kv_proj_t(x, g_kv, g_q, wkt, wv, wq, q_scale, bsz, tm, kt_tile):
    rows, d = x.shape
    seqlen = rows // bsz
    width = wv.shape[1]
    heads = width // LANES
    assert seqlen % tm == 0 and tm % kt_tile == 0
    nt = seqlen // tm
    parts = tm // kt_tile
    row_blk = lambda b, i: (b * nt + i, 0)
    resident = lambda shape: pl.BlockSpec(shape, lambda b, i: (0, 0), pipeline_mode=pl.Buffered(1))
    return pl.pallas_call(
        functools.partial(_kv_proj_t_kernel, heads=heads, q_scale=q_scale, parts=parts),
        grid=(bsz, nt),
        in_specs=[pl.BlockSpec((tm, d), row_blk),
                  resident((1, d)),
                  resident((1, d)),
                  resident((width, d)),
                  resident((d, width)),
                  resident((d, width))],
        out_specs=[pl.BlockSpec((1, width, tm), lambda b, i: (b, 0, i)),
                   pl.BlockSpec((1, parts, width, kt_tile), lambda b, i: (b, i, 0, 0)),
                   pl.BlockSpec((tm, width), row_blk),
                   pl.BlockSpec((1, heads, tm, LANES), lambda b, i: (b, 0, i, 0)),
                   pl.BlockSpec((tm, width), row_blk)],
        out_shape=[jax.ShapeDtypeStruct((bsz, width, seqlen), F32),
                   jax.ShapeDtypeStruct((bsz, seqlen // kt_tile, width, kt_tile), BF16),
                   jax.ShapeDtypeStruct((rows, width), F32),
                   jax.ShapeDtypeStruct((bsz, heads, seqlen, LANES), BF16),
                   jax.ShapeDtypeStruct((rows, width), BF16)],
        compiler_params=_params("parallel", "parallel", vmem_limit_bytes=DENSE_VMEM_LIMIT_BYTES),
        name="kv_proj_t",
    )(x, g_kv, g_q, wkt, wv, wq)


def _attn_sample_kernel(q_ref, ckt_ref, cv_ref, kn_ref, vn_ref, lam_ref, gsub_ref, o_ref, qs_sc, m_sc, l_sc, acc_sc,
                        *, lq, heads, past, lam_init):
    j = pl.program_id(1)

    @pl.when(j == 0)
    def _():
        for h in range(heads):
            qs_sc[h] = _stack_maps(q_ref[0, :, h * LANES:(h + 1) * LANES])
        m_sc[...] = jnp.full(m_sc.shape, NEG_INF, F32)
        l_sc[...] = jnp.zeros_like(l_sc)
        acc_sc[...] = jnp.zeros_like(acc_sc)

    hr = range(heads)

    def update(ss, vts):
        m_prev = [m_sc[h] for h in hr]
        m_new = [jnp.maximum(m_prev[h], jnp.max(ss[h], axis=1, keepdims=True)) for h in hr]
        alpha = [jnp.exp2(m_prev[h] - m_new[h]) for h in hr]
        ps = [jnp.exp2(ss[h] - m_new[h]) for h in hr]
        pv = [jnp.dot(ps[h].astype(BF16), vts[h], preferred_element_type=F32) for h in hr]
        for h in hr:
            l_sc[h] = alpha[h] * l_sc[h] + jnp.sum(ps[h], axis=1, keepdims=True)
            acc_sc[h] = alpha[h] * acc_sc[h] + pv[h]
            m_sc[h] = m_new[h]

    v_heads = pltpu.einshape("phd->hpd", cv_ref[0])
    update([jnp.dot(qs_sc[h], ckt_ref[0, h * LANES:(h + 1) * LANES, :].astype(BF16), preferred_element_type=F32)
            for h in hr],
           [v_heads[h].astype(BF16) for h in hr])

    @pl.when(j == pl.num_programs(1) - 1)
    def _():
        row = lax.broadcasted_iota(jnp.int32, (2 * lq, lq), 0)
        row = jnp.where(row >= lq, row - lq, row)
        col = lax.broadcasted_iota(jnp.int32, (2 * lq, lq), 1)
        mask = ((past + col) // CHUNK) <= ((past + row) // CHUNK)
        lam = _lambda(lam_ref, lam_init)
        hs = [slice(h * LANES, (h + 1) * LANES) for h in hr]
        update([jnp.where(mask, _dot_nt(qs_sc[h], kn_ref[0, :, hs[h]]), NEG_INF) for h in hr],
               [vn_ref[0, :, hs[h]] for h in hr])
        for h in hr:
            o = _finish_head(acc_sc[h], l_sc[h], lam, gsub_ref[:, hs[h]], lam_init, lq)
            o_ref[0, :, hs[h]] = o.astype(BF16)


def _attn_sample(q, cache_kt, cache_v, k_new, v_new, lam_vec, g_sub, lam_init, tkc):
    bsz, lq, width = q.shape
    past = cache_kt.shape[2]
    heads = width // LANES
    assert past % tkc == 0 and past % CHUNK == 0
    new_blk = pl.BlockSpec((1, lq, width), lambda b, j: (b, 0, 0))
    return pl.pallas_call(
        functools.partial(_attn_sample_kernel, lq=lq, heads=heads, past=past, lam_init=lam_init),
        grid=(bsz, past // tkc),
        in_specs=[new_blk,
                  pl.BlockSpec((1, width, tkc), lambda b, j: (b, 0, j)),
                  pl.BlockSpec((1, tkc, heads, LANES), lambda b, j: (b, j, 0, 0)),
                  new_blk,
                  new_blk,
                  pl.BlockSpec((4, HD_B), lambda b, j: (0, 0)),
                  pl.BlockSpec((1, width), lambda b, j: (0, 0))],
        out_specs=new_blk,
        out_shape=jax.ShapeDtypeStruct((bsz, lq, width), BF16),
        scratch_shapes=[pltpu.VMEM((heads, 2 * lq, LANES), BF16),
                        pltpu.VMEM((heads, 2 * lq, 1), F32),
                        pltpu.VMEM((heads, 2 * lq, 1), F32),
                        pltpu.VMEM((heads, 2 * lq, LANES), F32)],
        compiler_params=_params("parallel", "arbitrary"),
        name="attn_sample",
    )(q, cache_kt, cache_v, k_new, v_new, lam_vec, g_sub)


def _lambda_init(layer_idx):
    return 0.8 - 0.6 * math.exp(-0.3 * layer_idx)


def _trunk(x3, state, cache, p, mlstm_blk):
    bsz, seqlen, d = x3.shape
    rows = bsz * seqlen
    tm = min(ROW_TILE, rows)
    x = x3.reshape(rows, d)
    row = lambda g: g.reshape(1, -1)

    qk_w = H_A * p["dk"]
    v_w = H_A * p["dv"]
    w_in = p["w_in_a"]
    w_parts = [w_in[:, 0:qk_w], w_in[:, qk_w:2 * qk_w], w_in[:, 2 * qk_w:2 * qk_w + v_w],
               w_in[:, 2 * qk_w + v_w:2 * qk_w + 2 * v_w]]
    w_gate = jnp.pad(w_in[:, 2 * qk_w + 2 * v_w:], ((0, 0), (0, LANES - 2 * H_A)))
    ws = [w.astype(BF16) for w in w_parts] + [w_gate.astype(BF16)]
    specs = [(1.0, (BF16,)), (p["dk"] ** -0.5, (BF16,)), (1.0, (BF16,)), (1.0, (BF16,)), (1.0, (F32,))]
    q, k, v, o, gates = _norm_proj(x, row(p["g_mix_pre"][0]), ws, specs, min(DENSE_ROW_TILE, rows))

    bias = jnp.concatenate([p["b_i_a"], p["b_f_a"]]).astype(F32)
    b_row = jnp.pad(bias, (0, LANES - 2 * H_A)).reshape(1, LANES)
    b_col = bias.reshape(2 * H_A, 1)
    c0, n0, m0 = state
    m0b = jnp.broadcast_to(m0[:, :, None], (bsz, H_A, LANES))
    n0b = jnp.broadcast_to(n0[:, :, :, None], n0.shape + (LANES,))
    r3 = lambda a: a.reshape(bsz, seqlen, a.shape[-1])
    hg, c1, n1b, m1 = _mlstm(r3(q), r3(k), r3(v), r3(o), r3(gates), b_row, b_col, row(p["g_h_a"]), c0, n0b, m0b,
                             mlstm_blk, MLSTM_STREAMS, min(MLSTM_BLOCKS_PER_STEP, seqlen // mlstm_blk))
    n1 = n1b[:, :, :, 0]
    x = _mix_mlp(hg.reshape(rows, v_w), x, p["w_out_a"].astype(BF16), row(p["g_mix_post"][0]),
                 row(p["g_ffn_pre"][0]), p["w_up"][0].astype(BF16), p["w_down"][0].astype(BF16),
                 row(p["g_ffn_post"][0]), min(DENSE_ROW_TILE, rows), FF_TILE)

    kb_w = H_B * 2 * HD_B
    w_k = p["w_kv"][:, :kb_w]
    w_v = p["w_kv"][:, kb_w:].astype(BF16)
    q_scale = HD_B ** -0.5 * math.log2(math.e)
    w_q = p["w_q_b"].astype(BF16)
    g_q = row(p["g_mix_pre"][1])

    lam_init = _lambda_init(1)
    lam_vec = jnp.stack([p["lambda_q1"], p["lambda_k1"], p["lambda_q2"], p["lambda_k2"]]).astype(F32)
    g_sub = row(p["g_sub_b"])
    shape3 = (bsz, seqlen, kb_w)
    if cache is None:
        kt_new, kt_bf, v_new, v_bf, qb = _kv_proj_t(x, row(p["g_kv"]), g_q, w_k.T.astype(BF16), w_v, w_q, q_scale,
                                                    bsz, min(DENSE_ROW_TILE, seqlen), ATTN_TILE)
        k_out = kt_new.reshape(bsz, H_B, 2, HD_B, seqlen).transpose(0, 4, 1, 2, 3)
        att = _attn_prompt(qb.reshape(shape3), kt_bf, v_bf, lam_vec, g_sub, lam_init, ATTN_HEAD_GROUP,
                           min(ATTN_QUERY_TILES_PER_STEP, seqlen // ATTN_TILE))
    else:
        (qb,) = _norm_proj(x, g_q, [w_q], [(q_scale, (BF16,))], tm)
        k_new, k_bf, v_new, v_bf = _norm_proj(x, row(p["g_kv"]), [w_k.astype(BF16), w_v],
                                              [(1.0, (F32, BF16)), (1.0, (F32, BF16))], tm)
        k_out = k_new.reshape(bsz, seqlen, H_B, 2, HD_B)
        ck, cv = cache
        past = ck.shape[1]
        ckt = ck.transpose(0, 2, 3, 4, 1).reshape(bsz, kb_w, past)
        att = _attn_sample(qb.reshape(shape3), ckt, cv, k_bf.reshape(shape3), v_bf.reshape(shape3),
                           lam_vec, g_sub, lam_init, min(SAMPLE_KEY_TILE, past))
    x = _mix_mlp(att.reshape(rows, kb_w), x, p["w_o_b"].astype(BF16), row(p["g_mix_post"][1]),
                 row(p["g_ffn_pre"][1]), p["w_up"][1].astype(BF16), p["w_down"][1].astype(BF16),
                 row(p["g_ffn_post"][1]), min(DENSE_ROW_TILE, rows), FF_TILE)

    y = x.reshape(bsz, seqlen, d)
    v_out = v_new.reshape(bsz, seqlen, H_B, 2 * HD_B)
    return y, (c1[None], n1[None], m1[None, :, :, 0]), k_out, v_out


def kernel(x_prompt, x_sample, cache_k, cache_v, state_c, state_n, state_m, w_in_a, b_i_a, b_f_a, g_h_a, w_out_a,
           g_kv, w_kv, w_q_b, lambda_q1, lambda_k1, lambda_q2, lambda_k2, g_sub_b, w_o_b, g_mix_pre, g_mix_post,
           g_ffn_pre, g_ffn_post, w_up, w_down):
    assert w_in_a.shape[0] == 1 and w_q_b.shape[0] == 1 and g_mix_pre.shape[0] == 2
    dk, dv = state_c.shape[3], state_c.shape[4]
    p = {"dk": dk, "dv": dv,
         "w_in_a": w_in_a[0], "b_i_a": b_i_a[0], "b_f_a": b_f_a[0], "g_h_a": g_h_a[0], "w_out_a": w_out_a[0],
         "g_kv": g_kv, "w_kv": w_kv, "w_q_b": w_q_b[0],
         "lambda_q1": lambda_q1[0], "lambda_k1": lambda_k1[0], "lambda_q2": lambda_q2[0], "lambda_k2": lambda_k2[0],
         "g_sub_b": g_sub_b[0], "w_o_b": w_o_b[0], "g_mix_pre": g_mix_pre, "g_mix_post": g_mix_post,
         "g_ffn_pre": g_ffn_pre, "g_ffn_post": g_ffn_post, "w_up": w_up, "w_down": w_down}
    bsz = x_prompt.shape[0]
    zero_state = (jnp.zeros((bsz, H_A, dk, dv), F32), jnp.zeros((bsz, H_A, dk), F32), jnp.zeros((bsz, H_A), F32))
    y_p, (c_p, n_p, m_p), k_p, v_p = _trunk(x_prompt, zero_state, None, p, MLSTM_BLOCK)
    y_s, (c_s, n_s, m_s), k_s, v_s = _trunk(
        x_sample, (state_c[0].astype(F32), state_n[0].astype(F32), state_m[0].astype(F32)),
        (cache_k, cache_v), p, x_sample.shape[1])
    return (y_p, y_s, c_p, n_p, m_p, k_p, v_p, c_s, n_s, m_s, k_s, v_s)
```
